```python
import jax, jax.numpy as jnp
from jax import lax
import numpy as np

D_MODEL = 1024
BATCH = 8
SEQ = 4096
DEPTH = 2

D_MIX = D_MODEL
D_CONF = D_MIX // 2
D_SC = D_MIX - D_CONF
N_GROUPS_CONF = 8
N_GROUPS_SC = 8
CONF_KERNEL = 31
SC_KERNEL = 3
FFN_KERNEL = 3
D_FF = 2816
D_IN = 2 * D_CONF + 3 * D_SC
EPS = 1e-6

kernel_name = "hybrid_conformer_shortconv_convffn"


def rmsnorm(x, g):
    xf = x.astype(jnp.float32)
    y = xf * lax.rsqrt(jnp.mean(xf * xf, axis=-1, keepdims=True) + EPS)
    return (y * g.astype(jnp.float32)).astype(x.dtype)


def layernorm(x, g, b):
    xf = x.astype(jnp.float32)
    mu = jnp.mean(xf, axis=-1, keepdims=True)
    xc = xf - mu
    var = jnp.mean(xc * xc, axis=-1, keepdims=True)
    y = xc * lax.rsqrt(var + EPS)
    return (y * g.astype(jnp.float32) + b.astype(jnp.float32)).astype(x.dtype)


def causal_dwconv(x, w):
    k, c = w.shape
    return lax.conv_general_dilated(
        x, w[:, None, :].astype(x.dtype),
        window_strides=(1,), padding=[(k - 1, 0)],
        dimension_numbers=("NWC", "WIO", "NWC"),
        feature_group_count=c)


def setup_inputs(seed: int = 0) -> dict:
    key = jax.random.key(seed)
    ks = jax.random.split(key, 16)
    f32 = jnp.float32
    x = jax.random.normal(ks[0], (BATCH, SEQ, D_MODEL), f32)
    mix_norm_g = 1.0 + 0.02 * jax.random.normal(ks[1], (DEPTH, D_MODEL), f32)
    w_in = jax.random.normal(ks[2], (DEPTH, D_MODEL, D_IN), f32) * D_MODEL ** -0.5
    b_in = 0.02 * jax.random.normal(ks[3], (DEPTH, D_IN), f32)
    conv_a_w = jax.random.normal(ks[4], (DEPTH, CONF_KERNEL, D_CONF), f32) * CONF_KERNEL ** -0.5
    conv_a_b = 0.02 * jax.random.normal(ks[5], (DEPTH, D_CONF), f32)
    ln_a_g = 1.0 + 0.02 * jax.random.normal(ks[6], (DEPTH, D_CONF), f32)
    ln_a_b = 0.02 * jax.random.normal(ks[7], (DEPTH, D_CONF), f32)
    conv_b_w = jax.random.normal(ks[8], (DEPTH, SC_KERNEL, D_SC), f32) * SC_KERNEL ** -0.5
    w_out = jax.random.normal(ks[9], (DEPTH, D_MIX, D_MODEL), f32) * D_MIX ** -0.5
    ffn_norm_g = 1.0 + 0.02 * jax.random.normal(ks[10], (DEPTH, D_MODEL), f32)
    w_up = jax.random.normal(ks[11], (DEPTH, D_MODEL, 2 * D_FF), f32) * D_MODEL ** -0.5
    conv_f_w = jax.random.normal(ks[12], (DEPTH, FFN_KERNEL, 2 * D_FF), f32) * FFN_KERNEL ** -0.5
    w_down = jax.random.normal(ks[13], (DEPTH, D_FF, D_MODEL), f32) * D_FF ** -0.5
    final_norm_g = 1.0 + 0.02 * jax.random.normal(ks[14], (D_MODEL,), f32)
    return {"x": x, "mix_norm_g": mix_norm_g, "w_in": w_in, "b_in": b_in,
            "conv_a_w": conv_a_w, "conv_a_b": conv_a_b, "ln_a_g": ln_a_g, "ln_a_b": ln_a_b,
            "conv_b_w": conv_b_w, "w_out": w_out, "ffn_norm_g": ffn_norm_g, "w_up": w_up,
            "conv_f_w": conv_f_w, "w_down": w_down, "final_norm_g": final_norm_g}


def token_mixer(h, w_in, b_in, conv_a_w, conv_a_b, ln_a_g, ln_a_b, conv_b_w, w_out):
    u = jnp.einsum("bsd,de->bse", h, w_in) + b_in.astype(h.dtype)
    a_val, a_gate, g_b, g_c, v_sc = jnp.split(
        u, [D_CONF, 2 * D_CONF, 2 * D_CONF + D_SC, 2 * D_CONF + 2 * D_SC], axis=-1)
    a = a_val * jax.nn.sigmoid(a_gate)
    a = causal_dwconv(a, conv_a_w) + conv_a_b.astype(a.dtype)
    a = jax.nn.silu(layernorm(a, ln_a_g, ln_a_b))
    s = g_b * causal_dwconv(g_c * v_sc, conv_b_w)
    y = jnp.concatenate([a, s], axis=-1)
    return jnp.einsum("bse,ed->bsd", y, w_out)


def conv_ffn(h, w_up, conv_f_w, w_down):
    u = jnp.einsum("bsd,df->bsf", h, w_up)
    u = causal_dwconv(u, conv_f_w)
    gate, val = jnp.split(u, 2, axis=-1)
    return jnp.einsum("bsf,fd->bsd", jax.nn.silu(gate) * val, w_down)


def reference(x, mix_norm_g, w_in, b_in, conv_a_w, conv_a_b, ln_a_g, ln_a_b,
              conv_b_w, w_out, ffn_norm_g, w_up, conv_f_w, w_down, final_norm_g):
    for l in range(DEPTH):
        h = rmsnorm(x, mix_norm_g[l])
        x = x + token_mixer(h, w_in[l], b_in[l], conv_a_w[l], conv_a_b[l],
                            ln_a_g[l], ln_a_b[l], conv_b_w[l], w_out[l])
        h = rmsnorm(x, ffn_norm_g[l])
        x = x + conv_ffn(h, w_up[l], conv_f_w[l], w_down[l])
    return rmsnorm(x, final_norm_g)
```

```python
import functools

import jax
import jax.numpy as jnp
from jax import lax
from jax.experimental import pallas as pl
from jax.experimental.pallas import tpu as pltpu

EPS = 1e-6
LANES = 128
SUBLANES = 8

CONF_KERNEL = 31
SHORT_KERNEL = 3
FFN_CHUNK = 256

SEQ_TILE = 512
ROW_CHUNK = 32
HALO_A = 32
HALO_S = SUBLANES
VMEM_LIMIT_BYTES = 56 * 1024 * 1024


def _rmsnorm(x, g):
    ms = jnp.mean(x * x, axis=-1, keepdims=True)
    return x * lax.rsqrt(ms + EPS) * g


def _sigmoid(x):
    return 1.0 / (1.0 + jnp.exp(-x))


def _mixer_kernel(x_ref, g_ref, w_in_ref, b_in_ref, caw_ref, cab_ref, lng_ref, lnb_ref,
                  cbw_ref, w_out_ref, o_ref, aext_ref, sext_ref, gb_ref, y_ref):
    t = x_ref.shape[1]
    d_conf = cab_ref.shape[1]
    n_slab = d_conf // LANES

    @pl.when(pl.program_id(1) == 0)
    def _():
        aext_ref[:, 0:HALO_A, :] = jnp.zeros((n_slab, HALO_A, LANES), jnp.float32)
        sext_ref[:, 0:HALO_S, :] = jnp.zeros((n_slab, HALO_S, LANES), jnp.float32)

    x = x_ref[0]
    h = _rmsnorm(x, g_ref[...]).astype(jnp.bfloat16)
    u = jnp.dot(h, w_in_ref[...], preferred_element_type=jnp.float32) + b_in_ref[...]
    a = u[:, 0:d_conf] * _sigmoid(u[:, d_conf:2 * d_conf])
    p = u[:, 3 * d_conf:4 * d_conf] * u[:, 4 * d_conf:5 * d_conf]
    gb_ref[...] = u[:, 2 * d_conf:3 * d_conf]
    for c in range(n_slab):
        aext_ref[c, HALO_A:HALO_A + t, :] = a[:, c * LANES:(c + 1) * LANES]
        sext_ref[c, HALO_S:HALO_S + t, :] = p[:, c * LANES:(c + 1) * LANES]

    groups = ROW_CHUNK // SUBLANES

    def row_chunk(i, carry):
        r0 = pl.multiple_of(i * ROW_CHUNK, ROW_CHUNK)
        accs = []
        for c in range(n_slab):
            cs = slice(c * LANES, (c + 1) * LANES)
            bias = jnp.broadcast_to(cab_ref[:, cs], (SUBLANES, LANES))
            acc = [bias for _ in range(groups)]
            for k in range(CONF_KERNEL):
                wk = caw_ref[k, :, cs]
                off = HALO_A - (CONF_KERNEL - 1) + k
                for r in range(groups):
                    acc[r] = acc[r] + aext_ref[c, pl.ds(r0 + r * SUBLANES + off, SUBLANES), :] * wk
            accs.append(acc)
        for r in range(groups):
            rows = pl.ds(r0 + r * SUBLANES, SUBLANES)
            tot = accs[0][r]
            for c in range(1, n_slab):
                tot = tot + accs[c][r]
            mu = jnp.sum(tot, axis=-1, keepdims=True) * (1.0 / d_conf)
            xc = [accs[c][r] - mu for c in range(n_slab)]
            sq = xc[0] * xc[0]
            for c in range(1, n_slab):
                sq = sq + xc[c] * xc[c]
            var = jnp.sum(sq, axis=-1, keepdims=True) * (1.0 / d_conf)
            inv = lax.rsqrt(var + EPS)
            for c in range(n_slab):
                cs = slice(c * LANES, (c + 1) * LANES)
                v = xc[c] * inv * lng_ref[:, cs] + lnb_ref[:, cs]
                y_ref[rows, cs] = v * _sigmoid(v)
        for c in range(n_slab):
            cs = slice(c * LANES, (c + 1) * LANES)
            ws = [cbw_ref[k, :, cs] for k in range(SHORT_KERNEL)]
            for r in range(groups):
                rows = pl.ds(r0 + r * SUBLANES, SUBLANES)
                s = None
                for k in range(SHORT_KERNEL):
                    off = HALO_S - (SHORT_KERNEL - 1) + k
                    term = sext_ref[c, pl.ds(r0 + r * SUBLANES + off, SUBLANES), :] * ws[k]
                    s = term if s is None else s + term
                y_ref[rows, d_conf + c * LANES:d_conf + (c + 1) * LANES] = s * gb_ref[rows, cs]
        return carry

    lax.fori_loop(0, t // ROW_CHUNK, row_chunk, 0)

    aext_ref[:, 0:HALO_A, :] = aext_ref[:, t:t + HALO_A, :]
    sext_ref[:, 0:HALO_S, :] = sext_ref[:, t:t + HALO_S, :]

    y = y_ref[...].astype(jnp.bfloat16)
    o_ref[0] = x + jnp.dot(y, w_out_ref[...], preferred_element_type=jnp.float32)


def _ffn_kernel(x_ref, g_ref, w_up_ref, cfw_ref, w_down_ref, fg_ref, o_ref,
                h_ref, acc_ref, work_ref, halo_ref, act_ref, *, final_norm):
    t = x_ref.shape[1]
    n_chunks = w_up_ref.shape[0]
    slabs = 2 * FFN_CHUNK // LANES
    half = slabs // 2

    @pl.when(pl.program_id(1) == 0)
    def _():
        halo_ref[...] = jnp.zeros(halo_ref.shape, jnp.float32)

    x = x_ref[0]
    h_ref[...] = _rmsnorm(x, g_ref[...]).astype(jnp.bfloat16)
    acc_ref[...] = jnp.zeros(acc_ref.shape, jnp.float32)

    groups = ROW_CHUNK // SUBLANES

    def chunk(cc, carry):
        u = jnp.dot(h_ref[...], w_up_ref[cc], preferred_element_type=jnp.float32)
        for s in range(slabs):
            work_ref[s, 0:HALO_S, :] = halo_ref[cc * slabs + s]
            work_ref[s, HALO_S:HALO_S + t, :] = u[:, s * LANES:(s + 1) * LANES]

        def row_chunk(i, c2):
            r0 = pl.multiple_of(i * ROW_CHUNK, ROW_CHUNK)
            for s in range(half):
                wg = [cfw_ref[cc, k, :, s * LANES:(s + 1) * LANES] for k in range(SHORT_KERNEL)]
                wv = [cfw_ref[cc, k, :, (half + s) * LANES:(half + s + 1) * LANES]
                      for k in range(SHORT_KERNEL)]
                for r in range(groups):
                    gate = None
                    val = None
                    for k in range(SHORT_KERNEL):
                        rows = pl.ds(r0 + r * SUBLANES + HALO_S - (SHORT_KERNEL - 1) + k, SUBLANES)
                        tg = work_ref[s, rows, :] * wg[k]
                        tv = work_ref[half + s, rows, :] * wv[k]
                        gate = tg if gate is None else gate + tg
                        val = tv if val is None else val + tv
                    act_ref[pl.ds(r0 + r * SUBLANES, SUBLANES), s * LANES:(s + 1) * LANES] = (
                        gate * _sigmoid(gate) * val)
            return c2

        lax.fori_loop(0, t // ROW_CHUNK, row_chunk, 0)

        for s in range(slabs):
            halo_ref[cc * slabs + s] = work_ref[s, t:t + HALO_S, :]
        acc_ref[...] += jnp.dot(act_ref[...].astype(jnp.bfloat16), w_down_ref[cc],
                                preferred_element_type=jnp.float32)
        return carry

    lax.fori_loop(0, n_chunks, chunk, 0)

    out = x + acc_ref[...]
    if final_norm:
        out = _rmsnorm(out, fg_ref[...])
    o_ref[0] = out


def _const_spec(shape):
    zeros = (0,) * len(shape)
    return pl.BlockSpec(shape, lambda b, j: zeros, pipeline_mode=pl.Buffered(1))


def _mixer_call(x, g, w_in, b_in, caw, cab, lng, lnb, cbw, w_out):
    bsz, seq, d = x.shape
    d_conf = cab.shape[1]
    n_slab = d_conf // LANES
    t = SEQ_TILE
    xspec = pl.BlockSpec((1, t, d), lambda b, j: (b, j, 0))
    return pl.pallas_call(
        _mixer_kernel,
        grid=(bsz, seq // t),
        in_specs=[xspec, _const_spec(g.shape), _const_spec(w_in.shape), _const_spec(b_in.shape),
                  _const_spec(caw.shape), _const_spec(cab.shape), _const_spec(lng.shape),
                  _const_spec(lnb.shape), _const_spec(cbw.shape), _const_spec(w_out.shape)],
        out_specs=xspec,
        out_shape=jax.ShapeDtypeStruct(x.shape, x.dtype),
        scratch_shapes=[
            pltpu.VMEM((n_slab, HALO_A + t, LANES), jnp.float32),
            pltpu.VMEM((n_slab, HALO_S + t, LANES), jnp.float32),
            pltpu.VMEM((t, d_conf), jnp.float32),
            pltpu.VMEM((t, 2 * d_conf), jnp.float32),
        ],
        compiler_params=pltpu.CompilerParams(
            dimension_semantics=("arbitrary", "arbitrary"),
            vmem_limit_bytes=VMEM_LIMIT_BYTES),
        name="mixer",
    )(x, g, w_in, b_in, caw, cab, lng, lnb, cbw, w_out)


def _ffn_call(x, g, w_up, cfw, w_down, fg, final_norm):
    bsz, seq, d = x.shape
    n_chunks = w_up.shape[0]
    slabs = 2 * FFN_CHUNK // LANES
    t = SEQ_TILE
    xspec = pl.BlockSpec((1, t, d), lambda b, j: (b, j, 0))
    return pl.pallas_call(
        functools.partial(_ffn_kernel, final_norm=final_norm),
        grid=(bsz, seq // t),
        in_specs=[xspec, _const_spec(g.shape), _const_spec(w_up.shape), _const_spec(cfw.shape),
                  _const_spec(w_down.shape), _const_spec(fg.shape)],
        out_specs=xspec,
        out_shape=jax.ShapeDtypeStruct(x.shape, x.dtype),
        scratch_shapes=[
            pltpu.VMEM((t, d), jnp.bfloat16),
            pltpu.VMEM((t, d), jnp.float32),
            pltpu.VMEM((slabs, HALO_S + t, LANES), jnp.float32),
            pltpu.VMEM((n_chunks * slabs, HALO_S, LANES), jnp.float32),
            pltpu.VMEM((t, FFN_CHUNK), jnp.float32),
        ],
        compiler_params=pltpu.CompilerParams(
            dimension_semantics=("arbitrary", "arbitrary"),
            vmem_limit_bytes=VMEM_LIMIT_BYTES),
        name="ffn_final" if final_norm else "ffn",
    )(x, g, w_up, cfw, w_down, fg)


def _sublane_bcast(w):
    k, c = w.shape
    return jnp.broadcast_to(w[:, None, :], (k, SUBLANES, c))


def kernel(x, mix_norm_g, w_in, b_in, conv_a_w, conv_a_b, ln_a_g, ln_a_b, conv_b_w, w_out,
           ffn_norm_g, w_up, conv_f_w, w_down, final_norm_g):
    depth = w_in.shape[0]
    d_model = x.shape[-1]
    d_ff = w_down.shape[1]
    n_chunks = d_ff // FFN_CHUNK
    bf16 = jnp.bfloat16
    for l in range(depth):
        x = _mixer_call(
            x, mix_norm_g[l][None], w_in[l].astype(bf16), b_in[l][None],
            _sublane_bcast(conv_a_w[l]), conv_a_b[l][None], ln_a_g[l][None], ln_a_b[l][None],
            _sublane_bcast(conv_b_w[l]), w_out[l].astype(bf16))
        wu = w_up[l].reshape(d_model, 2, n_chunks, FFN_CHUNK).transpose(2, 0, 1, 3)
        wu = wu.reshape(n_chunks, d_model, 2 * FFN_CHUNK).astype(bf16)
        cf = conv_f_w[l].reshape(SHORT_KERNEL, 2, n_chunks, FFN_CHUNK).transpose(2, 0, 1, 3)
        cf = cf.reshape(n_chunks, SHORT_KERNEL, 2 * FFN_CHUNK)
        cf = jnp.broadcast_to(cf[:, :, None, :], (n_chunks, SHORT_KERNEL, SUBLANES, 2 * FFN_CHUNK))
        wd = w_down[l].reshape(n_chunks, FFN_CHUNK, d_model).astype(bf16)
        x = _ffn_call(x, ffn_norm_g[l][None], wu, cf, wd, final_norm_g[None],
                      final_norm=(l == depth - 1))
    return x
```

```python
import functools

import jax
import jax.numpy as jnp
from jax import lax
from jax.experimental import pallas as pl
from jax.experimental.pallas import tpu as pltpu

EPS = 1e-6
LANES = 128
SUBLANES = 8

CONF_KERNEL = 31
SHORT_KERNEL = 3
FFN_CHUNK = 256

SEQ_TILE = 512
ROW_CHUNK = 32
HALO_A = 32
HALO_S = SUBLANES
VMEM_LIMIT_BYTES = 56 * 1024 * 1024


def _rmsnorm(x, g):
    ms = jnp.mean(x * x, axis=-1, keepdims=True)
    return x * lax.rsqrt(ms + EPS) * g


def _sigmoid(x):
    return 1.0 / (1.0 + jnp.exp(-x))


def _mixer_kernel(x_ref, g_ref, w_in_ref, b_in_ref, caw_ref, cab_ref, lng_ref, lnb_ref,
                  cbw_ref, w_out_ref, o_ref, aext_ref, sext_ref, gb_ref, y_ref):
    t = x_ref.shape[1]
    d_conf = cab_ref.shape[1]
    n_slab = d_conf // LANES

    @pl.when(pl.program_id(1) == 0)
    def _():
        aext_ref[:, 0:HALO_A, :] = jnp.zeros((n_slab, HALO_A, LANES), jnp.float32)
        sext_ref[:, 0:HALO_S, :] = jnp.zeros((n_slab, HALO_S, LANES), jnp.float32)

    x = x_ref[0]
    h = _rmsnorm(x, g_ref[...]).astype(jnp.bfloat16)
    u = jnp.dot(h, w_in_ref[...], preferred_element_type=jnp.float32) + b_in_ref[...]
    a = u[:, 0:d_conf] * _sigmoid(u[:, d_conf:2 * d_conf])
    p = u[:, 3 * d_conf:4 * d_conf] * u[:, 4 * d_conf:5 * d_conf]
    gb_ref[...] = u[:, 2 * d_conf:3 * d_conf]
    for c in range(n_slab):
        aext_ref[c, HALO_A:HALO_A + t, :] = a[:, c * LANES:(c + 1) * LANES]
        sext_ref[c, HALO_S:HALO_S + t, :] = p[:, c * LANES:(c + 1) * LANES]

    groups = ROW_CHUNK // SUBLANES

    def row_chunk(i, carry):
        r0 = pl.multiple_of(i * ROW_CHUNK, ROW_CHUNK)
        accs = []
        for c in range(n_slab):
            cs = slice(c * LANES, (c + 1) * LANES)
            bias = jnp.broadcast_to(cab_ref[:, cs], (SUBLANES, LANES))
            acc = [bias for _ in range(groups)]
            for k in range(CONF_KERNEL):
                wk = caw_ref[k, :, cs]
                off = HALO_A - (CONF_KERNEL - 1) + k
                for r in range(groups):
                    acc[r] = acc[r] + aext_ref[c, pl.ds(r0 + r * SUBLANES + off, SUBLANES), :] * wk
            accs.append(acc)
        for r in range(groups):
            rows = pl.ds(r0 + r * SUBLANES, SUBLANES)
            tot = accs[0][r]
            for c in range(1, n_slab):
                tot = tot + accs[c][r]
            mu = jnp.sum(tot, axis=-1, keepdims=True) * (1.0 / d_conf)
            xc = [accs[c][r] - mu for c in range(n_slab)]
            sq = xc[0] * xc[0]
            for c in range(1, n_slab):
                sq = sq + xc[c] * xc[c]
            var = jnp.sum(sq, axis=-1, keepdims=True) * (1.0 / d_conf)
            inv = lax.rsqrt(var + EPS)
            for c in range(n_slab):
                cs = slice(c * LANES, (c + 1) * LANES)
                v = xc[c] * inv * lng_ref[:, cs] + lnb_ref[:, cs]
                y_ref[rows, cs] = v * _sigmoid(v)
        for c in range(n_slab):
            cs = slice(c * LANES, (c + 1) * LANES)
            ws = [cbw_ref[k, :, cs] for k in range(SHORT_KERNEL)]
            for r in range(groups):
                rows = pl.ds(r0 + r * SUBLANES, SUBLANES)
                s = None
                for k in range(SHORT_KERNEL):
                    off = HALO_S - (SHORT_KERNEL - 1) + k
                    term = sext_ref[c, pl.ds(r0 + r * SUBLANES + off, SUBLANES), :] * ws[k]
                    s = term if s is None else s + term
                y_ref[rows, d_conf + c * LANES:d_conf + (c + 1) * LANES] = s * gb_ref[rows, cs]
        return carry

    lax.fori_loop(0, t // ROW_CHUNK, row_chunk, 0)

    aext_ref[:, 0:HALO_A, :] = aext_ref[:, t:t + HALO_A, :]
    sext_ref[:, 0:HALO_S, :] = sext_ref[:, t:t + HALO_S, :]

    y = y_ref[...].astype(jnp.bfloat16)
    o_ref[0] = x + jnp.dot(y, w_out_ref[...], preferred_element_type=jnp.float32)


def _ffn_kernel(x_ref, g_ref, w_up_ref, cfw_ref, w_down_ref, fg_ref, o_ref,
                h_ref, work_ref, halo_ref, act_ref, *, final_norm):
    t = x_ref.shape[1]
    n_chunks = w_up_ref.shape[0]
    slabs = 2 * FFN_CHUNK // LANES
    half = slabs // 2
    pack_rows = 2 * SUBLANES

    @pl.when(pl.program_id(1) == 0)
    def _():
        halo_ref[...] = jnp.zeros(halo_ref.shape, jnp.float32)

    h_ref[...] = _rmsnorm(x_ref[0], g_ref[...]).astype(jnp.bfloat16)

    def up(cc):
        u = jnp.dot(h_ref[...], w_up_ref[cc], preferred_element_type=jnp.float32)
        for s in range(slabs):
            work_ref[cc % 2, s, 0:HALO_S, :] = halo_ref[cc * slabs + s]
            work_ref[cc % 2, s, HALO_S:HALO_S + t, :] = u[:, s * LANES:(s + 1) * LANES]

    def conv_gate(cc):
        work = work_ref.at[cc % 2]
        for s in range(half):
            wg = [cfw_ref[cc, k, :, s * LANES:(s + 1) * LANES] for k in range(SHORT_KERNEL)]
            wv = [cfw_ref[cc, k, :, (half + s) * LANES:(half + s + 1) * LANES]
                  for k in range(SHORT_KERNEL)]
            for r0 in range(0, t, pack_rows):
                parts = []
                for r in range(r0, r0 + pack_rows, SUBLANES):
                    gate = None
                    val = None
                    for k in range(SHORT_KERNEL):
                        rows = pl.ds(r + HALO_S - (SHORT_KERNEL - 1) + k, SUBLANES)
                        tg = work[s, rows, :] * wg[k]
                        tv = work[half + s, rows, :] * wv[k]
                        gate = tg if gate is None else gate + tg
                        val = tv if val is None else val + tv
                    parts.append(gate * _sigmoid(gate) * val)
                col = cc * FFN_CHUNK + s * LANES
                act_ref[r0:r0 + pack_rows, col:col + LANES] = (
                    jnp.concatenate(parts, axis=0).astype(jnp.bfloat16))
        for s in range(slabs):
            halo_ref[cc * slabs + s] = work[s, t:t + HALO_S, :]

    up(0)
    for cc in range(n_chunks):
        if cc + 1 < n_chunks:
            up(cc + 1)
        conv_gate(cc)

    out = x_ref[0] + jnp.dot(act_ref[...], w_down_ref[...], preferred_element_type=jnp.float32)
    if final_norm:
        out = _rmsnorm(out, fg_ref[...])
    o_ref[0] = out


def _const_spec(shape):
    zeros = (0,) * len(shape)
    return pl.BlockSpec(shape, lambda b, j: zeros, pipeline_mode=pl.Buffered(1))


def _mixer_call(x, g, w_in, b_in, caw, cab, lng, lnb, cbw, w_out):
    bsz, seq, d = x.shape
    d_conf = cab.shape[1]
    n_slab = d_conf // LANES
    t = SEQ_TILE
    xspec = pl.BlockSpec((1, t, d), lambda b, j: (b, j, 0))
    return pl.pallas_call(
        _mixer_kernel,
        grid=(bsz, seq // t),
        in_specs=[xspec, _const_spec(g.shape), _const_spec(w_in.shape), _const_spec(b_in.shape),
                  _const_spec(caw.shape), _const_spec(cab.shape), _const_spec(lng.shape),
                  _const_spec(lnb.shape), _const_spec(cbw.shape), _const_spec(w_out.shape)],
        out_specs=xspec,
        out_shape=jax.ShapeDtypeStruct(x.shape, x.dtype),
        scratch_shapes=[
            pltpu.VMEM((n_slab, HALO_A + t, LANES), jnp.float32),
            pltpu.VMEM((n_slab, HALO_S + t, LANES), jnp.float32),
            pltpu.VMEM((t, d_conf), jnp.float32),
            pltpu.VMEM((t, 2 * d_conf), jnp.float32),
        ],
        compiler_params=pltpu.CompilerParams(
            dimension_semantics=("arbitrary", "arbitrary"),
            vmem_limit_bytes=VMEM_LIMIT_BYTES),
        name="mixer",
    )(x, g, w_in, b_in, caw, cab, lng, lnb, cbw, w_out)


def _ffn_call(x, g, w_up, cfw, w_down, fg, final_norm):
    bsz, seq, d = x.shape
    n_chunks = w_up.shape[0]
    slabs = 2 * FFN_CHUNK // LANES
    t = SEQ_TILE
    xspec = pl.BlockSpec((1, t, d), lambda b, j: (b, j, 0))
    return pl.pallas_call(
        functools.partial(_ffn_kernel, final_norm=final_norm),
        grid=(bsz, seq // t),
        in_specs=[xspec, _const_spec(g.shape), _const_spec(w_up.shape), _const_spec(cfw.shape),
                  _const_spec(w_down.shape), _const_spec(fg.shape)],
        out_specs=xspec,
        out_shape=jax.ShapeDtypeStruct(x.shape, x.dtype),
        scratch_shapes=[
            pltpu.VMEM((t, d), jnp.bfloat16),
            pltpu.VMEM((2, slabs, HALO_S + t, LANES), jnp.float32),
            pltpu.VMEM((n_chunks * slabs, HALO_S, LANES), jnp.float32),
            pltpu.VMEM((t, n_chunks * FFN_CHUNK), jnp.bfloat16),
        ],
        compiler_params=pltpu.CompilerParams(
            dimension_semantics=("arbitrary", "arbitrary"),
            vmem_limit_bytes=VMEM_LIMIT_BYTES),
        name="ffn_final" if final_norm else "ffn",
    )(x, g, w_up, cfw, w_down, fg)


def _sublane_bcast(w):
    k, c = w.shape
    return jnp.broadcast_to(w[:, None, :], (k, SUBLANES, c))


def kernel(x, mix_norm_g, w_in, b_in, conv_a_w, conv_a_b, ln_a_g, ln_a_b, conv_b_w, w_out,
           ffn_norm_g, w_up, conv_f_w, w_down, final_norm_g):
    depth = w_in.shape[0]
    d_model = x.shape[-1]
    d_ff = w_down.shape[1]
    n_chunks = d_ff // FFN_CHUNK
    bf16 = jnp.bfloat16
    for l in range(depth):
        x = _mixer_call(
            x, mix_norm_g[l][None], w_in[l].astype(bf16), b_in[l][None],
            _sublane_bcast(conv_a_w[l]), conv_a_b[l][None], ln_a_g[l][None], ln_a_b[l][None],
            _sublane_bcast(conv_b_w[l]), w_out[l].astype(bf16))
        wu = w_up[l].reshape(d_model, 2, n_chunks, FFN_CHUNK).transpose(2, 0, 1, 3)
        wu = wu.reshape(n_chunks, d_model, 2 * FFN_CHUNK).astype(bf16)
        cf = conv_f_w[l].reshape(SHORT_KERNEL, 2, n_chunks, FFN_CHUNK).transpose(2, 0, 1, 3)
        cf = cf.reshape(n_chunks, SHORT_KERNEL, 2 * FFN_CHUNK)
        cf = jnp.broadcast_to(cf[:, :, None, :], (n_chunks, SHORT_KERNEL, SUBLANES, 2 * FFN_CHUNK))
        wd = w_down[l].astype(bf16)
        x = _ffn_call(x, ffn_norm_g[l][None], wu, cf, wd, final_norm_g[None],
                      final_norm=(l == depth - 1))
    return x
```

```python
import functools

import jax
import jax.numpy as jnp
from jax import lax
from jax.experimental import pallas as pl
from jax.experimental.pallas import tpu as pltpu

EPS = 1e-6
LANES = 128
SUBLANES = 8

CONF_KERNEL = 31
SHORT_KERNEL = 3
FFN_CHUNK = 256

SEQ_TILE = 512
HALO_A = 32
HALO_S = SUBLANES
VMEM_LIMIT_BYTES = 56 * 1024 * 1024


def _rmsnorm(x, g):
    ms = jnp.mean(x * x, axis=-1, keepdims=True)
    return x * lax.rsqrt(ms + EPS) * g


def _sigmoid(x):
    return 1.0 / (1.0 + jnp.exp(-x))


def _mixer_kernel(xa_ref, xc_ref, g_ref, w_in_ref, b_in_ref, caw_ref, cab_ref, lng_ref, lnb_ref,
                  cbw_ref, w_out_ref, o_ref, aext_ref, sext_ref, gb_ref, y_ref, h_ref, *, tiles_per_seq):
    t = xa_ref.shape[1]
    d_conf = cab_ref.shape[1]
    n_slab = d_conf // LANES
    pack_rows = 2 * SUBLANES
    step = pl.program_id(0)
    slot_a = lax.rem(step, 2)
    slot_b = 1 - slot_a

    @pl.when(step == 0)
    def _():
        aext_ref[...] = jnp.zeros(aext_ref.shape, aext_ref.dtype)
        sext_ref[...] = jnp.zeros(sext_ref.shape, sext_ref.dtype)
        gb_ref[...] = jnp.zeros(gb_ref.shape, gb_ref.dtype)
        y_ref[...] = jnp.zeros(y_ref.shape, y_ref.dtype)

    @pl.when(lax.rem(step + tiles_per_seq - 1, tiles_per_seq) == 0)
    def _():
        aext_ref[slot_b, :, 0:HALO_A, :] = jnp.zeros((n_slab, HALO_A, LANES), jnp.float32)
        sext_ref[slot_b, :, 0:HALO_S, :] = jnp.zeros((n_slab, HALO_S, LANES), jnp.float32)

    refs = (xa_ref, xc_ref, g_ref, w_in_ref, b_in_ref, caw_ref, cab_ref, lng_ref, lnb_ref,
            cbw_ref, w_out_ref, o_ref, aext_ref, sext_ref, gb_ref, y_ref, h_ref)

    @pl.when(slot_a == 0)
    def _():
        _mixer_stages(0, 1, *refs)

    @pl.when(slot_a == 1)
    def _():
        _mixer_stages(1, 0, *refs)


def _mixer_stages(slot_a, slot_b, xa_ref, xc_ref, g_ref, w_in_ref, b_in_ref, caw_ref, cab_ref,
                  lng_ref, lnb_ref, cbw_ref, w_out_ref, o_ref, aext_ref, sext_ref, gb_ref, y_ref,
                  h_ref):
    t = xa_ref.shape[1]
    d_conf = cab_ref.shape[1]
    n_slab = d_conf // LANES
    pack_rows = 2 * SUBLANES

    aext = aext_ref.at[slot_b]
    sext = sext_ref.at[slot_b]
    gb = gb_ref.at[slot_b]
    y = y_ref.at[slot_b]
    cslices = [slice(c * LANES, (c + 1) * LANES) for c in range(n_slab)]

    def in_proj(n):
        cols = slice(n * d_conf, (n + 1) * d_conf)
        return (jnp.dot(h_ref[...], w_in_ref[:, cols], preferred_element_type=jnp.float32)
                + b_in_ref[:, cols])

    def to_slabs(ref, halo, val):
        for c in range(n_slab):
            ref[slot_a, c, halo:halo + t, :] = val[:, cslices[c]]

    def from_slabs(ref, halo):
        return jnp.concatenate([ref[slot_a, c, halo:halo + t, :] for c in range(n_slab)], axis=1)

    def conv_rows(r_start, r_stop):
        ln_g = [lng_ref[:, cs] for cs in cslices]
        ln_b = [lnb_ref[:, cs] for cs in cslices]
        conv_b = [jnp.broadcast_to(cab_ref[:, cs], (SUBLANES, LANES)) for cs in cslices]
        short_w = [[cbw_ref[k, :, cs] for k in range(SHORT_KERNEL)] for cs in cslices]
        for r0 in range(r_start, r_stop, pack_rows):
            groups = range(r0, r0 + pack_rows, SUBLANES)
            acc = [[conv_b[c] for _ in groups] for c in range(n_slab)]
            for c in range(n_slab):
                for k in range(CONF_KERNEL):
                    wk = caw_ref[k, :, cslices[c]]
                    off = HALO_A - (CONF_KERNEL - 1) + k
                    for i, r in enumerate(groups):
                        acc[c][i] = acc[c][i] + aext[c, pl.ds(r + off, SUBLANES), :] * wk
            normed = [[] for _ in range(n_slab)]
            for i, r in enumerate(groups):
                tot = acc[0][i]
                for c in range(1, n_slab):
                    tot = tot + acc[c][i]
                mu = jnp.sum(tot, axis=-1, keepdims=True) * (1.0 / d_conf)
                xc = [acc[c][i] - mu for c in range(n_slab)]
                sq = xc[0] * xc[0]
                for c in range(1, n_slab):
                    sq = sq + xc[c] * xc[c]
                var = jnp.sum(sq, axis=-1, keepdims=True) * (1.0 / d_conf)
                inv = lax.rsqrt(var + EPS)
                for c in range(n_slab):
                    v = xc[c] * inv * ln_g[c] + ln_b[c]
                    normed[c].append(v * _sigmoid(v))
            for c in range(n_slab):
                y[r0:r0 + pack_rows, cslices[c]] = (
                    jnp.concatenate(normed[c], axis=0).astype(jnp.bfloat16))
            for c in range(n_slab):
                outs = []
                for r in groups:
                    sacc = None
                    for k in range(SHORT_KERNEL):
                        off = HALO_S - (SHORT_KERNEL - 1) + k
                        term = sext[c, pl.ds(r + off, SUBLANES), :] * short_w[c][k]
                        sacc = term if sacc is None else sacc + term
                    outs.append(sacc * gb[r:r + SUBLANES, cslices[c]])
                y[r0:r0 + pack_rows, d_conf + c * LANES:d_conf + (c + 1) * LANES] = (
                    jnp.concatenate(outs, axis=0).astype(jnp.bfloat16))

    def out_proj(n, width):
        cols = slice(n * width, (n + 1) * width)
        o_ref[0, :, cols] = xc_ref[0, :, cols] + jnp.dot(
            y_ref[slot_a], w_out_ref[:, cols], preferred_element_type=jnp.float32)

    n_blocks = 8
    rows = t // n_blocks
    out_w = o_ref.shape[2] // 2
    out_proj(0, out_w)
    h_ref[...] = _rmsnorm(xa_ref[0], g_ref[...]).astype(jnp.bfloat16)
    conv_rows(0 * rows, 1 * rows)
    out_proj(1, out_w)
    conv_rows(1 * rows, 2 * rows)
    to_slabs(aext_ref, HALO_A, _sigmoid(in_proj(1)))
    conv_rows(2 * rows, 3 * rows)
    to_slabs(aext_ref, HALO_A, in_proj(0) * from_slabs(aext_ref, HALO_A))
    conv_rows(3 * rows, 4 * rows)
    to_slabs(sext_ref, HALO_S, in_proj(3))
    conv_rows(4 * rows, 5 * rows)
    to_slabs(sext_ref, HALO_S, in_proj(4) * from_slabs(sext_ref, HALO_S))
    conv_rows(5 * rows, 6 * rows)
    gb_ref[slot_a] = in_proj(2)
    conv_rows(6 * rows, 7 * rows)
    conv_rows(7 * rows, 8 * rows)
    aext_ref[slot_a, :, 0:HALO_A, :] = aext_ref[slot_b, :, t:t + HALO_A, :]
    sext_ref[slot_a, :, 0:HALO_S, :] = sext_ref[slot_b, :, t:t + HALO_S, :]


def _ffn_kernel(x_ref, g_ref, w_up_ref, cfw_ref, w_down_ref, fg_ref, o_ref,
                h_ref, work_ref, halo_ref, act_ref, *, final_norm):
    t = x_ref.shape[1]
    d_ff = w_down_ref.shape[0]
    n_chunks = d_ff // FFN_CHUNK
    slabs = 2 * FFN_CHUNK // LANES
    half = slabs // 2
    pack_rows = 2 * SUBLANES

    def up_cols(cc, s):
        return (s // half) * d_ff + cc * FFN_CHUNK + (s % half) * LANES

    @pl.when(pl.program_id(1) == 0)
    def _():
        halo_ref[...] = jnp.zeros(halo_ref.shape, jnp.float32)

    h_ref[...] = _rmsnorm(x_ref[0], g_ref[...]).astype(jnp.bfloat16)

    def up(cc):
        for part in range(2):
            c0 = up_cols(cc, part * half)
            u = jnp.dot(h_ref[...], w_up_ref[:, c0:c0 + FFN_CHUNK],
                        preferred_element_type=jnp.float32)
            for s in range(half):
                slab = part * half + s
                work_ref[cc % 2, slab, 0:HALO_S, :] = halo_ref[cc * slabs + slab]
                work_ref[cc % 2, slab, HALO_S:HALO_S + t, :] = u[:, s * LANES:(s + 1) * LANES]

    def conv_gate(cc):
        work = work_ref.at[cc % 2]
        for s in range(half):
            cg = up_cols(cc, s)
            cv = up_cols(cc, half + s)
            wg = [cfw_ref[k, :, cg:cg + LANES] for k in range(SHORT_KERNEL)]
            wv = [cfw_ref[k, :, cv:cv + LANES] for k in range(SHORT_KERNEL)]
            for r0 in range(0, t, pack_rows):
                parts = []
                for r in range(r0, r0 + pack_rows, SUBLANES):
                    gate = None
                    val = None
                    for k in range(SHORT_KERNEL):
                        rows = pl.ds(r + HALO_S - (SHORT_KERNEL - 1) + k, SUBLANES)
                        tg = work[s, rows, :] * wg[k]
                        tv = work[half + s, rows, :] * wv[k]
                        gate = tg if gate is None else gate + tg
                        val = tv if val is None else val + tv
                    parts.append(gate * _sigmoid(gate) * val)
                col = cc * FFN_CHUNK + s * LANES
                act_ref[r0:r0 + pack_rows, col:col + LANES] = (
                    jnp.concatenate(parts, axis=0).astype(jnp.bfloat16))
        for s in range(slabs):
            halo_ref[cc * slabs + s] = work[s, t:t + HALO_S, :]

    up(0)
    for cc in range(n_chunks):
        if cc + 1 < n_chunks:
            up(cc + 1)
        conv_gate(cc)

    out = x_ref[0] + jnp.dot(act_ref[...], w_down_ref[...], preferred_element_type=jnp.float32)
    if final_norm:
        out = _rmsnorm(out, fg_ref[...])
    o_ref[0] = out


def _const_spec(shape, n_grid):
    zeros = (0,) * len(shape)
    if n_grid == 1:
        index_map = lambda s: zeros
    else:
        index_map = lambda b, j: zeros
    return pl.BlockSpec(shape, index_map, pipeline_mode=pl.Buffered(1))


def _mixer_call(x, g, w_in, b_in, caw, cab, lng, lnb, cbw, w_out):
    bsz, seq, d = x.shape
    d_conf = cab.shape[1]
    n_slab = d_conf // LANES
    t = SEQ_TILE
    tiles_per_seq = seq // t
    n_tiles = bsz * tiles_per_seq
    lag = 2

    def tile_index(q):
        q = jnp.clip(q, 0, n_tiles - 1)
        return (q // tiles_per_seq, lax.rem(q, tiles_per_seq), 0)

    consts = [g, w_in, b_in, caw, cab, lng, lnb, cbw, w_out]
    return pl.pallas_call(
        functools.partial(_mixer_kernel, tiles_per_seq=tiles_per_seq),
        grid=(n_tiles + lag,),
        in_specs=[pl.BlockSpec((1, t, d), lambda s: tile_index(s)),
                  pl.BlockSpec((1, t, d), lambda s: tile_index(s - lag))]
                 + [_const_spec(c.shape, 1) for c in consts],
        out_specs=pl.BlockSpec((1, t, d), lambda s: tile_index(s - lag)),
        out_shape=jax.ShapeDtypeStruct(x.shape, x.dtype),
        scratch_shapes=[
            pltpu.VMEM((2, n_slab, HALO_A + t, LANES), jnp.float32),
            pltpu.VMEM((2, n_slab, HALO_S + t, LANES), jnp.float32),
            pltpu.VMEM((2, t, d_conf), jnp.float32),
            pltpu.VMEM((2, t, 2 * d_conf), jnp.bfloat16),
            pltpu.VMEM((t, d), jnp.bfloat16),
        ],
        compiler_params=pltpu.CompilerParams(
            dimension_semantics=("arbitrary",),
            vmem_limit_bytes=VMEM_LIMIT_BYTES),
        name="mixer",
    )(x, x, *consts)


def _ffn_call(x, g, w_up, cfw, w_down, fg, final_norm):
    bsz, seq, d = x.shape
    n_chunks = w_down.shape[0] // FFN_CHUNK
    slabs = 2 * FFN_CHUNK // LANES
    t = SEQ_TILE
    xspec = pl.BlockSpec((1, t, d), lambda b, j: (b, j, 0))
    return pl.pallas_call(
        functools.partial(_ffn_kernel, final_norm=final_norm),
        grid=(bsz, seq // t),
        in_specs=[xspec] + [_const_spec(c.shape, 2) for c in (g, w_up, cfw, w_down, fg)],
        out_specs=xspec,
        out_shape=jax.ShapeDtypeStruct(x.shape, x.dtype),
        scratch_shapes=[
            pltpu.VMEM((t, d), jnp.bfloat16),
            pltpu.VMEM((2, slabs, HALO_S + t, LANES), jnp.float32),
            pltpu.VMEM((n_chunks * slabs, HALO_S, LANES), jnp.float32),
            pltpu.VMEM((t, n_chunks * FFN_CHUNK), jnp.bfloat16),
        ],
        compiler_params=pltpu.CompilerParams(
            dimension_semantics=("arbitrary", "arbitrary"),
            vmem_limit_bytes=VMEM_LIMIT_BYTES),
        name="ffn_final" if final_norm else "ffn",
    )(x, g, w_up, cfw, w_down, fg)


def _sublane_bcast(w):
    k, c = w.shape
    return jnp.broadcast_to(w[:, None, :], (k, SUBLANES, c))


def kernel(x, mix_norm_g, w_in, b_in, conv_a_w, conv_a_b, ln_a_g, ln_a_b, conv_b_w, w_out,
           ffn_norm_g, w_up, conv_f_w, w_down, final_norm_g):
    depth = w_in.shape[0]
    bf16 = jnp.bfloat16
    for l in range(depth):
        x = _mixer_call(
            x, mix_norm_g[l][None], w_in[l].astype(bf16), b_in[l][None],
            _sublane_bcast(conv_a_w[l]), conv_a_b[l][None], ln_a_g[l][None], ln_a_b[l][None],
            _sublane_bcast(conv_b_w[l]), w_out[l].astype(bf16))
        x = _ffn_call(x, ffn_norm_g[l][None], w_up[l].astype(bf16), _sublane_bcast(conv_f_w[l]),
                      w_down[l].astype(bf16), final_norm_g[None], final_norm=(l == depth - 1))
    return x
```

```python
import functools

import jax
import jax.numpy as jnp
from jax import lax
from jax.experimental import pallas as pl
from jax.experimental.pallas import tpu as pltpu

EPS = 1e-6
LANES = 128
SUBLANES = 8

CONF_KERNEL = 31
SHORT_KERNEL = 3
FFN_CHUNK = 256

SEQ_TILE = 512
HALO_PAIRS = 16
HALO_S = SUBLANES
VMEM_LIMIT_BYTES = 56 * 1024 * 1024


def _rmsnorm(x, g):
    ms = jnp.mean(x * x, axis=-1, keepdims=True)
    return x * lax.rsqrt(ms + EPS) * g


def _sigmoid(x):
    return 1.0 / (1.0 + jnp.exp(-x))


def _mixer_kernel(xa_ref, xc_ref, g_ref, w_in_ref, b_in_ref, caw_ref, cab_ref, lng_ref, lnb_ref,
                  cbw_ref, w_out_ref, o_ref, ae_ref, ao_ref, atmp_ref, cv_ref, sext_ref, gb_ref, y_ref,
                  h_ref, *, tiles_per_seq):
    n_slab = cab_ref.shape[1] // LANES
    step = pl.program_id(0)
    slot_a = lax.rem(step, 2)
    slot_b = 1 - slot_a

    @pl.when(step == 0)
    def _():
        for ref in (ae_ref, ao_ref, atmp_ref, cv_ref, sext_ref, gb_ref, y_ref):
            ref[...] = jnp.zeros(ref.shape, ref.dtype)

    @pl.when(lax.rem(step + tiles_per_seq - 1, tiles_per_seq) == 0)
    def _():
        ae_ref[slot_b, :, 0:HALO_PAIRS, :] = jnp.zeros((n_slab, HALO_PAIRS, LANES), jnp.uint32)
        ao_ref[slot_b, :, 0:HALO_PAIRS, :] = jnp.zeros((n_slab, HALO_PAIRS, LANES), jnp.uint32)
        sext_ref[slot_b, :, 0:HALO_S, :] = jnp.zeros((n_slab, HALO_S, LANES), jnp.float32)

    @pl.when(lax.rem(step, tiles_per_seq) == 0)
    def _():
        atmp_ref[:, 0:SUBLANES, :] = jnp.zeros((n_slab, SUBLANES, LANES), jnp.float32)

    refs = (xa_ref, xc_ref, g_ref, w_in_ref, b_in_ref, caw_ref, cab_ref, lng_ref, lnb_ref,
            cbw_ref, w_out_ref, o_ref, ae_ref, ao_ref, atmp_ref, cv_ref, sext_ref, gb_ref, y_ref,
            h_ref)

    @pl.when(slot_a == 0)
    def _():
        _mixer_stages(0, 1, *refs)

    @pl.when(slot_a == 1)
    def _():
        _mixer_stages(1, 0, *refs)


def _mixer_stages(slot_a, slot_b, xa_ref, xc_ref, g_ref, w_in_ref, b_in_ref, caw_ref, cab_ref,
                  lng_ref, lnb_ref, cbw_ref, w_out_ref, o_ref, ae_ref, ao_ref, atmp_ref, cv_ref,
                  sext_ref, gb_ref, y_ref, h_ref):
    t = xa_ref.shape[1]
    d_conf = cab_ref.shape[1]
    n_slab = d_conf // LANES
    pack_rows = 2 * SUBLANES

    ae = ae_ref.at[slot_b]
    ao = ao_ref.at[slot_b]
    sext = sext_ref.at[slot_b]
    gb = gb_ref.at[slot_b]
    y = y_ref.at[slot_b]
    cslices = [slice(c * LANES, (c + 1) * LANES) for c in range(n_slab)]

    def in_proj(n):
        cols = slice(n * d_conf, (n + 1) * d_conf)
        return (jnp.dot(h_ref[...], w_in_ref[:, cols], preferred_element_type=jnp.float32)
                + b_in_ref[:, cols])

    def to_slabs(ref, halo, val):
        for c in range(n_slab):
            ref[c, halo:halo + t, :] = val[:, cslices[c]]

    def from_slabs(ref, halo):
        return jnp.concatenate([ref[c, halo:halo + t, :] for c in range(n_slab)], axis=1)

    def pack_pairs():
        for c in range(n_slab):
            for r in range(0, t, pack_rows):
                dst = pl.ds(HALO_PAIRS + r // 2, SUBLANES)
                even = atmp_ref[c, pl.ds(SUBLANES + r, pack_rows), :]
                odd = atmp_ref[c, pl.ds(SUBLANES + r - 1, pack_rows), :]
                ae_ref[slot_a, c, dst, :] = pltpu.bitcast(even.astype(jnp.bfloat16), jnp.uint32)
                ao_ref[slot_a, c, dst, :] = pltpu.bitcast(odd.astype(jnp.bfloat16), jnp.uint32)
            atmp_ref[c, 0:SUBLANES, :] = atmp_ref[c, t:t + SUBLANES, :]

    def conv31_tile():
        n_groups = 4
        span = n_groups * pack_rows

        def body(it, carry):
            pair0 = pl.multiple_of(it * (span // 2), span // 2)
            for c in range(n_slab):
                acc = [None] * n_groups
                for k in range(CONF_KERNEL):
                    wk = caw_ref[k, :, cslices[c]].astype(jnp.float32)
                    shift = CONF_KERNEL - 1 - k
                    src = ae if shift % 2 == 0 else ao
                    for i in range(n_groups):
                        pair = (i * pack_rows - shift + shift % 2) // 2
                        words = src[c, pl.ds(pair0 + HALO_PAIRS + pair, SUBLANES), :]
                        term = pltpu.bitcast(words, jnp.bfloat16).astype(jnp.float32) * wk
                        acc[i] = term if acc[i] is None else acc[i] + term
                bias = jnp.broadcast_to(cab_ref[:, cslices[c]], (pack_rows, LANES))
                for i in range(n_groups):
                    row = pl.multiple_of(it * span + i * pack_rows, pack_rows)
                    cv_ref[c, pl.ds(row, pack_rows), :] = acc[i] + bias
            return carry

        lax.fori_loop(0, t // span, body, 0)

    def conv_rows(r_start, r_stop):
        ln_g = [lng_ref[:, cs] for cs in cslices]
        ln_b = [lnb_ref[:, cs] for cs in cslices]
        short_w = [[cbw_ref[k, :, cs] for k in range(SHORT_KERNEL)] for cs in cslices]
        for r in range(r_start, r_stop, pack_rows):
            conv = [cv_ref[c, r:r + pack_rows, :] for c in range(n_slab)]
            tot = conv[0]
            for c in range(1, n_slab):
                tot = tot + conv[c]
            mu = jnp.sum(tot, axis=-1, keepdims=True) * (1.0 / d_conf)
            xc = [conv[c] - mu for c in range(n_slab)]
            sq = xc[0] * xc[0]
            for c in range(1, n_slab):
                sq = sq + xc[c] * xc[c]
            var = jnp.sum(sq, axis=-1, keepdims=True) * (1.0 / d_conf)
            inv = lax.rsqrt(var + EPS)
            for c in range(n_slab):
                v = xc[c] * inv * ln_g[c] + ln_b[c]
                y[r:r + pack_rows, cslices[c]] = (v * _sigmoid(v)).astype(jnp.bfloat16)
            for c in range(n_slab):
                outs = []
                for r8 in range(r, r + pack_rows, SUBLANES):
                    sacc = None
                    for k in range(SHORT_KERNEL):
                        off = HALO_S - (SHORT_KERNEL - 1) + k
                        term = sext[c, pl.ds(r8 + off, SUBLANES), :] * short_w[c][k]
                        sacc = term if sacc is None else sacc + term
                    outs.append(sacc * gb[r8:r8 + SUBLANES, cslices[c]])
                y[r:r + pack_rows, d_conf + c * LANES:d_conf + (c + 1) * LANES] = (
                    jnp.concatenate(outs, axis=0).astype(jnp.bfloat16))

    def out_proj(n, width):
        cols = slice(n * width, (n + 1) * width)
        o_ref[0, :, cols] = xc_ref[0, :, cols] + jnp.dot(
            y_ref[slot_a], w_out_ref[:, cols], preferred_element_type=jnp.float32)

    n_blocks = 8
    rows = t // n_blocks
    out_w = o_ref.shape[2] // 2
    conv31_tile()
    out_proj(0, out_w)
    h_ref[...] = _rmsnorm(xa_ref[0], g_ref[...]).astype(jnp.bfloat16)
    conv_rows(0 * rows, 1 * rows)
    out_proj(1, out_w)
    conv_rows(1 * rows, 2 * rows)
    to_slabs(atmp_ref, SUBLANES, _sigmoid(in_proj(1)))
    conv_rows(2 * rows, 3 * rows)
    to_slabs(atmp_ref, SUBLANES, in_proj(0) * from_slabs(atmp_ref, SUBLANES))
    pack_pairs()
    conv_rows(3 * rows, 4 * rows)
    to_slabs(sext_ref.at[slot_a], HALO_S, in_proj(3))
    conv_rows(4 * rows, 5 * rows)
    to_slabs(sext_ref.at[slot_a], HALO_S,
             in_proj(4) * from_slabs(sext_ref.at[slot_a], HALO_S))
    conv_rows(5 * rows, 6 * rows)
    gb_ref[slot_a] = in_proj(2)
    conv_rows(6 * rows, 7 * rows)
    conv_rows(7 * rows, 8 * rows)
    half_t = t // 2
    ae_ref[slot_a, :, 0:HALO_PAIRS, :] = ae_ref[slot_b, :, half_t:half_t + HALO_PAIRS, :]
    ao_ref[slot_a, :, 0:HALO_PAIRS, :] = ao_ref[slot_b, :, half_t:half_t + HALO_PAIRS, :]
    sext_ref[slot_a, :, 0:HALO_S, :] = sext_ref[slot_b, :, t:t + HALO_S, :]


def _ffn_kernel(x_ref, g_ref, w_up_ref, cfw_ref, w_down_ref, fg_ref, o_ref,
                h_ref, work_ref, halo_ref, act_ref, *, final_norm):
    t = x_ref.shape[1]
    d_ff = w_down_ref.shape[0]
    n_chunks = d_ff // FFN_CHUNK
    slabs = 2 * FFN_CHUNK // LANES
    half = slabs // 2
    pack_rows = 2 * SUBLANES

    def up_cols(cc, s):
        return (s // half) * d_ff + cc * FFN_CHUNK + (s % half) * LANES

    @pl.when(pl.program_id(1) == 0)
    def _():
        halo_ref[...] = jnp.zeros(halo_ref.shape, jnp.float32)

    h_ref[...] = _rmsnorm(x_ref[0], g_ref[...]).astype(jnp.bfloat16)

    def up(cc):
        for part in range(2):
            c0 = up_cols(cc, part * half)
            u = jnp.dot(h_ref[...], w_up_ref[:, c0:c0 + FFN_CHUNK],
                        preferred_element_type=jnp.float32)
            for s in range(half):
                slab = part * half + s
                work_ref[cc % 2, slab, 0:HALO_S, :] = halo_ref[cc * slabs + slab]
                work_ref[cc % 2, slab, HALO_S:HALO_S + t, :] = u[:, s * LANES:(s + 1) * LANES]

    def conv_gate(cc):
        work = work_ref.at[cc % 2]
        for s in range(half):
            cg = up_cols(cc, s)
            cv = up_cols(cc, half + s)
            wg = [cfw_ref[k, :, cg:cg + LANES] for k in range(SHORT_KERNEL)]
            wv = [cfw_ref[k, :, cv:cv + LANES] for k in range(SHORT_KERNEL)]
            for r0 in range(0, t, pack_rows):
                parts = []
                for r in range(r0, r0 + pack_rows, SUBLANES):
                    gate = None
                    val = None
                    for k in range(SHORT_KERNEL):
                        rows = pl.ds(r + HALO_S - (SHORT_KERNEL - 1) + k, SUBLANES)
                        tg = work[s, rows, :] * wg[k]
                        tv = work[half + s, rows, :] * wv[k]
                        gate = tg if gate is None else gate + tg
                        val = tv if val is None else val + tv
                    parts.append(gate * _sigmoid(gate) * val)
                col = cc * FFN_CHUNK + s * LANES
                act_ref[r0:r0 + pack_rows, col:col + LANES] = (
                    jnp.concatenate(parts, axis=0).astype(jnp.bfloat16))
        for s in range(slabs):
            halo_ref[cc * slabs + s] = work[s, t:t + HALO_S, :]

    up(0)
    for cc in range(n_chunks):
        if cc + 1 < n_chunks:
            up(cc + 1)
        conv_gate(cc)

    out = x_ref[0] + jnp.dot(act_ref[...], w_down_ref[...], preferred_element_type=jnp.float32)
    if final_norm:
        out = _rmsnorm(out, fg_ref[...])
    o_ref[0] = out


def _const_spec(shape, n_grid):
    zeros = (0,) * len(shape)
    if n_grid == 1:
        index_map = lambda s: zeros
    else:
        index_map = lambda b, j: zeros
    return pl.BlockSpec(shape, index_map, pipeline_mode=pl.Buffered(1))


def _mixer_call(x, g, w_in, b_in, caw, cab, lng, lnb, cbw, w_out):
    bsz, seq, d = x.shape
    d_conf = cab.shape[1]
    n_slab = d_conf // LANES
    t = SEQ_TILE
    tiles_per_seq = seq // t
    n_tiles = bsz * tiles_per_seq
    lag = 2

    def tile_index(q):
        q = jnp.clip(q, 0, n_tiles - 1)
        return (q // tiles_per_seq, lax.rem(q, tiles_per_seq), 0)

    consts = [g, w_in, b_in, caw, cab, lng, lnb, cbw, w_out]
    return pl.pallas_call(
        functools.partial(_mixer_kernel, tiles_per_seq=tiles_per_seq),
        grid=(n_tiles + lag,),
        in_specs=[pl.BlockSpec((1, t, d), lambda s: tile_index(s)),
                  pl.BlockSpec((1, t, d), lambda s: tile_index(s - lag))]
                 + [_const_spec(c.shape, 1) for c in consts],
        out_specs=pl.BlockSpec((1, t, d), lambda s: tile_index(s - lag)),
        out_shape=jax.ShapeDtypeStruct(x.shape, x.dtype),
        scratch_shapes=[
            pltpu.VMEM((2, n_slab, HALO_PAIRS + t // 2, LANES), jnp.uint32),
            pltpu.VMEM((2, n_slab, HALO_PAIRS + t // 2, LANES), jnp.uint32),
            pltpu.VMEM((n_slab, SUBLANES + t, LANES), jnp.float32),
            pltpu.VMEM((n_slab, t, LANES), jnp.float32),
            pltpu.VMEM((2, n_slab, HALO_S + t, LANES), jnp.float32),
            pltpu.VMEM((2, t, d_conf), jnp.float32),
            pltpu.VMEM((2, t, 2 * d_conf), jnp.bfloat16),
            pltpu.VMEM((t, d), jnp.bfloat16),
        ],
        compiler_params=pltpu.CompilerParams(
            dimension_semantics=("arbitrary",),
            vmem_limit_bytes=VMEM_LIMIT_BYTES),
        name="mixer",
    )(x, x, *consts)


def _ffn_call(x, g, w_up, cfw, w_down, fg, final_norm):
    bsz, seq, d = x.shape
    n_chunks = w_down.shape[0] // FFN_CHUNK
    slabs = 2 * FFN_CHUNK // LANES
    t = SEQ_TILE
    xspec = pl.BlockSpec((1, t, d), lambda b, j: (b, j, 0))
    return pl.pallas_call(
        functools.partial(_ffn_kernel, final_norm=final_norm),
        grid=(bsz, seq // t),
        in_specs=[xspec] + [_const_spec(c.shape, 2) for c in (g, w_up, cfw, w_down, fg)],
        out_specs=xspec,
        out_shape=jax.ShapeDtypeStruct(x.shape, x.dtype),
        scratch_shapes=[
            pltpu.VMEM((t, d), jnp.bfloat16),
            pltpu.VMEM((2, slabs, HALO_S + t, LANES), jnp.float32),
            pltpu.VMEM((n_chunks * slabs, HALO_S, LANES), jnp.float32),
            pltpu.VMEM((t, n_chunks * FFN_CHUNK), jnp.bfloat16),
        ],
        compiler_params=pltpu.CompilerParams(
            dimension_semantics=("arbitrary", "arbitrary"),
            vmem_limit_bytes=VMEM_LIMIT_BYTES),
        name="ffn_final" if final_norm else "ffn",
    )(x, g, w_up, cfw, w_down, fg)


def _sublane_bcast(w, rows=SUBLANES):
    k, c = w.shape
    return jnp.broadcast_to(w[:, None, :], (k, rows, c))


def kernel(x, mix_norm_g, w_in, b_in, conv_a_w, conv_a_b, ln_a_g, ln_a_b, conv_b_w, w_out,
           ffn_norm_g, w_up, conv_f_w, w_down, final_norm_g):
    depth = w_in.shape[0]
    bf16 = jnp.bfloat16
    for l in range(depth):
        x = _mixer_call(
            x, mix_norm_g[l][None], w_in[l].astype(bf16), b_in[l][None],
            _sublane_bcast(conv_a_w[l].astype(bf16), 2 * SUBLANES), conv_a_b[l][None],
            ln_a_g[l][None], ln_a_b[l][None],
            _sublane_bcast(conv_b_w[l]), w_out[l].astype(bf16))
        x = _ffn_call(x, ffn_norm_g[l][None], w_up[l].astype(bf16), _sublane_bcast(conv_f_w[l]),
                      w_down[l].astype(bf16), final_norm_g[None], final_norm=(l == depth - 1))
    return x
```

```python
import functools

import jax
import jax.numpy as jnp
from jax import lax
from jax.experimental import pallas as pl
from jax.experimental.pallas import tpu as pltpu

EPS = 1e-6
LANES = 128
SUBLANES = 8

CONF_KERNEL = 31
SHORT_KERNEL = 3
FFN_CHUNK = 256

SEQ_TILE = 512
HALO_PAIRS = 16
HALO_S = SUBLANES
VMEM_LIMIT_BYTES = 56 * 1024 * 1024


def _rmsnorm(x, g):
    ms = jnp.mean(x * x, axis=-1, keepdims=True)
    return x * lax.rsqrt(ms + EPS) * g


def _sigmoid(x):
    return 1.0 / (1.0 + jnp.exp(-x))


def _mixer_kernel(xa_ref, xc_ref, g_ref, w_in_ref, b_in_ref, caw_ref, cab_ref, lng_ref, lnb_ref,
                  cbw_ref, w_out_ref, o_ref, ae_ref, ao_ref, atmp_ref, cv_ref, sext_ref, gb_ref, y_ref,
                  h_ref, *, tiles_per_seq):
    n_slab = cab_ref.shape[1] // LANES
    step = pl.program_id(0)
    slot_a = lax.rem(step, 2)
    slot_b = 1 - slot_a

    @pl.when(step == 0)
    def _():
        for ref in (ae_ref, ao_ref, atmp_ref, cv_ref, sext_ref, gb_ref, y_ref):
            ref[...] = jnp.zeros(ref.shape, ref.dtype)

    @pl.when(lax.rem(step + tiles_per_seq - 1, tiles_per_seq) == 0)
    def _():
        ae_ref[slot_b, :, 0:HALO_PAIRS, :] = jnp.zeros((n_slab, HALO_PAIRS, LANES), jnp.uint32)
        ao_ref[slot_b, :, 0:HALO_PAIRS, :] = jnp.zeros((n_slab, HALO_PAIRS, LANES), jnp.uint32)
        sext_ref[slot_b, :, 0:HALO_S, :] = jnp.zeros((n_slab, HALO_S, LANES), jnp.float32)

    @pl.when(lax.rem(step, tiles_per_seq) == 0)
    def _():
        atmp_ref[:, 0:SUBLANES, :] = jnp.zeros((n_slab, SUBLANES, LANES), jnp.float32)

    refs = (xa_ref, xc_ref, g_ref, w_in_ref, b_in_ref, caw_ref, cab_ref, lng_ref, lnb_ref,
            cbw_ref, w_out_ref, o_ref, ae_ref, ao_ref, atmp_ref, cv_ref, sext_ref, gb_ref, y_ref,
            h_ref)

    @pl.when(slot_a == 0)
    def _():
        _mixer_stages(0, 1, *refs)

    @pl.when(slot_a == 1)
    def _():
        _mixer_stages(1, 0, *refs)


def _mixer_stages(slot_a, slot_b, xa_ref, xc_ref, g_ref, w_in_ref, b_in_ref, caw_ref, cab_ref,
                  lng_ref, lnb_ref, cbw_ref, w_out_ref, o_ref, ae_ref, ao_ref, atmp_ref, cv_ref,
                  sext_ref, gb_ref, y_ref, h_ref):
    t = xa_ref.shape[1]
    d_conf = cab_ref.shape[1]
    n_slab = d_conf // LANES
    pack_rows = 2 * SUBLANES

    ae = ae_ref.at[slot_b]
    ao = ao_ref.at[slot_b]
    sext = sext_ref.at[slot_b]
    gb = gb_ref.at[slot_b]
    y = y_ref.at[slot_b]
    cslices = [slice(c * LANES, (c + 1) * LANES) for c in range(n_slab)]

    def in_proj(n):
        cols = slice(n * d_conf, (n + 1) * d_conf)
        return (jnp.dot(h_ref[...], w_in_ref[:, cols], preferred_element_type=jnp.float32)
                + b_in_ref[:, cols])

    def to_slabs(ref, halo, val):
        for c in range(n_slab):
            ref[c, halo:halo + t, :] = val[:, cslices[c]]

    def from_slabs(ref, halo):
        return jnp.concatenate([ref[c, halo:halo + t, :] for c in range(n_slab)], axis=1)

    def pack_pairs():
        for c in range(n_slab):
            for r in range(0, t, pack_rows):
                dst = pl.ds(HALO_PAIRS + r // 2, SUBLANES)
                even = atmp_ref[c, pl.ds(SUBLANES + r, pack_rows), :]
                odd = atmp_ref[c, pl.ds(SUBLANES + r - 1, pack_rows), :]
                ae_ref[slot_a, c, dst, :] = pltpu.bitcast(even.astype(jnp.bfloat16), jnp.uint32)
                ao_ref[slot_a, c, dst, :] = pltpu.bitcast(odd.astype(jnp.bfloat16), jnp.uint32)
            atmp_ref[c, 0:SUBLANES, :] = atmp_ref[c, t:t + SUBLANES, :]

    def conv31_tile():
        n_groups = 4
        span = n_groups * pack_rows

        def body(it, carry):
            pair0 = pl.multiple_of(it * (span // 2), span // 2)
            for c in range(n_slab):
                acc = [None] * n_groups
                for k in range(CONF_KERNEL):
                    wk = caw_ref[k, :, cslices[c]].astype(jnp.float32)
                    shift = CONF_KERNEL - 1 - k
                    src = ae if shift % 2 == 0 else ao
                    for i in range(n_groups):
                        pair = (i * pack_rows - shift + shift % 2) // 2
                        words = src[c, pl.ds(pair0 + HALO_PAIRS + pair, SUBLANES), :]
                        term = pltpu.bitcast(words, jnp.bfloat16).astype(jnp.float32) * wk
                        acc[i] = term if acc[i] is None else acc[i] + term
                bias = jnp.broadcast_to(cab_ref[:, cslices[c]], (pack_rows, LANES))
                for i in range(n_groups):
                    row = pl.multiple_of(it * span + i * pack_rows, pack_rows)
                    cv_ref[c, pl.ds(row, pack_rows), :] = acc[i] + bias
            return carry

        lax.fori_loop(0, t // span, body, 0)

    def conv_rows(r_start, r_stop):
        ln_g = [lng_ref[:, cs] for cs in cslices]
        ln_b = [lnb_ref[:, cs] for cs in cslices]
        short_w = [[cbw_ref[k, :, cs] for k in range(SHORT_KERNEL)] for cs in cslices]
        for r in range(r_start, r_stop, pack_rows):
            conv = [cv_ref[c, r:r + pack_rows, :] for c in range(n_slab)]
            tot = conv[0]
            for c in range(1, n_slab):
                tot = tot + conv[c]
            mu = jnp.sum(tot, axis=-1, keepdims=True) * (1.0 / d_conf)
            xc = [conv[c] - mu for c in range(n_slab)]
            sq = xc[0] * xc[0]
            for c in range(1, n_slab):
                sq = sq + xc[c] * xc[c]
            var = jnp.sum(sq, axis=-1, keepdims=True) * (1.0 / d_conf)
            inv = lax.rsqrt(var + EPS)
            for c in range(n_slab):
                v = xc[c] * inv * ln_g[c] + ln_b[c]
                y[r:r + pack_rows, cslices[c]] = (v * _sigmoid(v)).astype(jnp.bfloat16)
            for c in range(n_slab):
                outs = []
                for r8 in range(r, r + pack_rows, SUBLANES):
                    sacc = None
                    for k in range(SHORT_KERNEL):
                        off = HALO_S - (SHORT_KERNEL - 1) + k
                        term = sext[c, pl.ds(r8 + off, SUBLANES), :] * short_w[c][k]
                        sacc = term if sacc is None else sacc + term
                    outs.append(sacc * gb[r8:r8 + SUBLANES, cslices[c]])
                y[r:r + pack_rows, d_conf + c * LANES:d_conf + (c + 1) * LANES] = (
                    jnp.concatenate(outs, axis=0).astype(jnp.bfloat16))

    def out_proj(n, width):
        cols = slice(n * width, (n + 1) * width)
        o_ref[0, :, cols] = xc_ref[0, :, cols] + jnp.dot(
            y_ref[slot_a], w_out_ref[:, cols], preferred_element_type=jnp.float32)

    n_blocks = 8
    rows = t // n_blocks
    out_w = o_ref.shape[2] // 2
    conv31_tile()
    out_proj(0, out_w)
    h_ref[...] = _rmsnorm(xa_ref[0], g_ref[...]).astype(jnp.bfloat16)
    conv_rows(0 * rows, 1 * rows)
    out_proj(1, out_w)
    conv_rows(1 * rows, 2 * rows)
    to_slabs(atmp_ref, SUBLANES, _sigmoid(in_proj(1)))
    conv_rows(2 * rows, 3 * rows)
    to_slabs(atmp_ref, SUBLANES, in_proj(0) * from_slabs(atmp_ref, SUBLANES))
    pack_pairs()
    conv_rows(3 * rows, 4 * rows)
    to_slabs(sext_ref.at[slot_a], HALO_S, in_proj(3))
    conv_rows(4 * rows, 5 * rows)
    to_slabs(sext_ref.at[slot_a], HALO_S,
             in_proj(4) * from_slabs(sext_ref.at[slot_a], HALO_S))
    conv_rows(5 * rows, 6 * rows)
    gb_ref[slot_a] = in_proj(2)
    conv_rows(6 * rows, 7 * rows)
    conv_rows(7 * rows, 8 * rows)
    half_t = t // 2
    ae_ref[slot_a, :, 0:HALO_PAIRS, :] = ae_ref[slot_b, :, half_t:half_t + HALO_PAIRS, :]
    ao_ref[slot_a, :, 0:HALO_PAIRS, :] = ao_ref[slot_b, :, half_t:half_t + HALO_PAIRS, :]
    sext_ref[slot_a, :, 0:HALO_S, :] = sext_ref[slot_b, :, t:t + HALO_S, :]


def _ffn_kernel(xa_ref, xc_ref, g_ref, w_up_ref, cfw_ref, w_down_ref, fg_ref, o_ref,
                h_ref, work_ref, halo_ref, act_ref, *, final_norm, tiles_per_seq):
    step = pl.program_id(0)
    slot_a = lax.rem(step, 2)

    @pl.when(step == 0)
    def _():
        act_ref[...] = jnp.zeros(act_ref.shape, act_ref.dtype)

    @pl.when(lax.rem(step, tiles_per_seq) == 0)
    def _():
        halo_ref[...] = jnp.zeros(halo_ref.shape, jnp.float32)

    refs = (xa_ref, xc_ref, g_ref, w_up_ref, cfw_ref, w_down_ref, fg_ref, o_ref,
            h_ref, work_ref, halo_ref, act_ref)

    @pl.when(slot_a == 0)
    def _():
        _ffn_stages(0, 1, final_norm, *refs)

    @pl.when(slot_a == 1)
    def _():
        _ffn_stages(1, 0, final_norm, *refs)


def _ffn_stages(slot_a, slot_c, final_norm, xa_ref, xc_ref, g_ref, w_up_ref, cfw_ref, w_down_ref,
                fg_ref, o_ref, h_ref, work_ref, halo_ref, act_ref):
    t = xa_ref.shape[1]
    d_ff = w_down_ref.shape[0]
    n_chunks = d_ff // FFN_CHUNK
    slabs = 2 * FFN_CHUNK // LANES
    half = slabs // 2
    pack_rows = 2 * SUBLANES

    def up_cols(cc, s):
        return (s // half) * d_ff + cc * FFN_CHUNK + (s % half) * LANES

    def up(cc):
        for part in range(2):
            c0 = up_cols(cc, part * half)
            u = jnp.dot(h_ref[...], w_up_ref[:, c0:c0 + FFN_CHUNK],
                        preferred_element_type=jnp.float32)
            for s in range(half):
                slab = part * half + s
                work_ref[cc % 2, slab, 0:HALO_S, :] = halo_ref[cc * slabs + slab]
                work_ref[cc % 2, slab, HALO_S:HALO_S + t, :] = u[:, s * LANES:(s + 1) * LANES]

    def conv_gate(cc):
        work = work_ref.at[cc % 2]
        for s in range(half):
            cg = up_cols(cc, s)
            cv = up_cols(cc, half + s)
            wg = [cfw_ref[k, :, cg:cg + LANES] for k in range(SHORT_KERNEL)]
            wv = [cfw_ref[k, :, cv:cv + LANES] for k in range(SHORT_KERNEL)]
            for r0 in range(0, t, pack_rows):
                parts = []
                for r in range(r0, r0 + pack_rows, SUBLANES):
                    gate = None
                    val = None
                    for k in range(SHORT_KERNEL):
                        rows = pl.ds(r + HALO_S - (SHORT_KERNEL - 1) + k, SUBLANES)
                        tg = work[s, rows, :] * wg[k]
                        tv = work[half + s, rows, :] * wv[k]
                        gate = tg if gate is None else gate + tg
                        val = tv if val is None else val + tv
                    parts.append(gate * _sigmoid(gate) * val)
                col = cc * FFN_CHUNK + s * LANES
                act_ref[slot_a, r0:r0 + pack_rows, col:col + LANES] = (
                    jnp.concatenate(parts, axis=0).astype(jnp.bfloat16))
        for s in range(slabs):
            halo_ref[cc * slabs + s] = work[s, t:t + HALO_S, :]

    def down(n, width):
        cols = slice(n * width, (n + 1) * width)
        o_ref[0, :, cols] = xc_ref[0, :, cols] + jnp.dot(
            act_ref[slot_c], w_down_ref[:, cols], preferred_element_type=jnp.float32)

    n_down = 4
    down_w = o_ref.shape[2] // n_down
    down(0, down_w)
    h_ref[...] = _rmsnorm(xa_ref[0], g_ref[...]).astype(jnp.bfloat16)
    up(0)
    for cc in range(n_chunks):
        if cc + 1 < n_chunks:
            up(cc + 1)
        conv_gate(cc)
    for n in range(1, n_down):
        down(n, down_w)
    if final_norm:
        o_ref[0] = _rmsnorm(o_ref[0], fg_ref[...])


def _const_spec(shape):
    zeros = (0,) * len(shape)
    return pl.BlockSpec(shape, lambda s: zeros, pipeline_mode=pl.Buffered(1))


def _tile_specs(x, lag):
    bsz, seq, d = x.shape
    t = SEQ_TILE
    tiles_per_seq = seq // t
    n_tiles = bsz * tiles_per_seq

    def tile_index(q):
        q = jnp.clip(q, 0, n_tiles - 1)
        return (q // tiles_per_seq, lax.rem(q, tiles_per_seq), 0)

    lead = pl.BlockSpec((1, t, d), lambda s: tile_index(s))
    lagged = pl.BlockSpec((1, t, d), lambda s: tile_index(s - lag))
    return n_tiles + lag, tiles_per_seq, lead, lagged


def _mixer_call(x, g, w_in, b_in, caw, cab, lng, lnb, cbw, w_out):
    d = x.shape[2]
    d_conf = cab.shape[1]
    n_slab = d_conf // LANES
    t = SEQ_TILE
    n_steps, tiles_per_seq, lead, lagged = _tile_specs(x, lag=2)
    consts = [g, w_in, b_in, caw, cab, lng, lnb, cbw, w_out]
    return pl.pallas_call(
        functools.partial(_mixer_kernel, tiles_per_seq=tiles_per_seq),
        grid=(n_steps,),
        in_specs=[lead, lagged] + [_const_spec(c.shape) for c in consts],
        out_specs=lagged,
        out_shape=jax.ShapeDtypeStruct(x.shape, x.dtype),
        scratch_shapes=[
            pltpu.VMEM((2, n_slab, HALO_PAIRS + t // 2, LANES), jnp.uint32),
            pltpu.VMEM((2, n_slab, HALO_PAIRS + t // 2, LANES), jnp.uint32),
            pltpu.VMEM((n_slab, SUBLANES + t, LANES), jnp.float32),
            pltpu.VMEM((n_slab, t, LANES), jnp.float32),
            pltpu.VMEM((2, n_slab, HALO_S + t, LANES), jnp.float32),
            pltpu.VMEM((2, t, d_conf), jnp.float32),
            pltpu.VMEM((2, t, 2 * d_conf), jnp.bfloat16),
            pltpu.VMEM((t, d), jnp.bfloat16),
        ],
        compiler_params=pltpu.CompilerParams(
            dimension_semantics=("arbitrary",),
            vmem_limit_bytes=VMEM_LIMIT_BYTES),
        name="mixer",
    )(x, x, *consts)


def _ffn_call(x, g, w_up, cfw, w_down, fg, final_norm):
    d = x.shape[2]
    d_ff = w_down.shape[0]
    slabs = 2 * FFN_CHUNK // LANES
    t = SEQ_TILE
    n_steps, tiles_per_seq, lead, lagged = _tile_specs(x, lag=1)
    consts = [g, w_up, cfw, w_down, fg]
    return pl.pallas_call(
        functools.partial(_ffn_kernel, final_norm=final_norm, tiles_per_seq=tiles_per_seq),
        grid=(n_steps,),
        in_specs=[lead, lagged] + [_const_spec(c.shape) for c in consts],
        out_specs=lagged,
        out_shape=jax.ShapeDtypeStruct(x.shape, x.dtype),
        scratch_shapes=[
            pltpu.VMEM((t, d), jnp.bfloat16),
            pltpu.VMEM((2, slabs, HALO_S + t, LANES), jnp.float32),
            pltpu.VMEM((d_ff // FFN_CHUNK * slabs, HALO_S, LANES), jnp.float32),
            pltpu.VMEM((2, t, d_ff), jnp.bfloat16),
        ],
        compiler_params=pltpu.CompilerParams(
            dimension_semantics=("arbitrary",),
            vmem_limit_bytes=VMEM_LIMIT_BYTES),
        name="ffn_final" if final_norm else "ffn",
    )(x, x, *consts)


def _sublane_bcast(w, rows=SUBLANES):
    k, c = w.shape
    return jnp.broadcast_to(w[:, None, :], (k, rows, c))


def kernel(x, mix_norm_g, w_in, b_in, conv_a_w, conv_a_b, ln_a_g, ln_a_b, conv_b_w, w_out,
           ffn_norm_g, w_up, conv_f_w, w_down, final_norm_g):
    depth = w_in.shape[0]
    bf16 = jnp.bfloat16
    for l in range(depth):
        x = _mixer_call(
            x, mix_norm_g[l][None], w_in[l].astype(bf16), b_in[l][None],
            _sublane_bcast(conv_a_w[l].astype(bf16), 2 * SUBLANES), conv_a_b[l][None],
            ln_a_g[l][None], ln_a_b[l][None],
            _sublane_bcast(conv_b_w[l]), w_out[l].astype(bf16))
        x = _ffn_call(x, ffn_norm_g[l][None], w_up[l].astype(bf16), _sublane_bcast(conv_f_w[l]),
                      w_down[l].astype(bf16), final_norm_g[None], final_norm=(l == depth - 1))
    return x
```

```python
import functools

import jax
import jax.numpy as jnp
from jax import lax
from jax.experimental import pallas as pl
from jax.experimental.pallas import tpu as pltpu

EPS = 1e-6
LANES = 128
SUBLANES = 8

CONF_KERNEL = 31
SHORT_KERNEL = 3
FFN_CHUNK = 256

MIXER_TILE = 512
FFN_TILE = 512
HALO_PAIRS = 16
HALO_S = SUBLANES
VMEM_LIMIT_BYTES = 56 * 1024 * 1024


def _rmsnorm(x, g):
    ms = jnp.mean(x * x, axis=-1, keepdims=True)
    return x * lax.rsqrt(ms + EPS) * g


def _sigmoid(x):
    return 1.0 / (1.0 + jnp.exp(-x))


def _mixer_kernel(xa_ref, xc_ref, g_ref, w_in_ref, b_in_ref, caw_ref, cab_ref, lng_ref, lnb_ref,
                  cbw_ref, w_out_ref, o_ref, ae_ref, ao_ref, atmp_ref, cv_ref, sext_ref, gb_ref, y_ref,
                  h_ref, *, tiles_per_seq):
    n_slab = cab_ref.shape[1] // LANES
    step = pl.program_id(0)
    slot_a = lax.rem(step, 2)
    slot_b = 1 - slot_a

    @pl.when(step == 0)
    def _():
        for ref in (ae_ref, ao_ref, atmp_ref, cv_ref, sext_ref, gb_ref, y_ref):
            ref[...] = jnp.zeros(ref.shape, ref.dtype)

    @pl.when(lax.rem(step + tiles_per_seq - 1, tiles_per_seq) == 0)
    def _():
        ae_ref[slot_b, :, 0:HALO_PAIRS, :] = jnp.zeros((n_slab, HALO_PAIRS, LANES), jnp.uint32)
        ao_ref[slot_b, :, 0:HALO_PAIRS, :] = jnp.zeros((n_slab, HALO_PAIRS, LANES), jnp.uint32)
        sext_ref[slot_b, :, 0:HALO_S, :] = jnp.zeros((n_slab, HALO_S, LANES), jnp.float32)

    @pl.when(lax.rem(step, tiles_per_seq) == 0)
    def _():
        atmp_ref[:, 0:SUBLANES, :] = jnp.zeros((n_slab, SUBLANES, LANES), jnp.float32)

    refs = (xa_ref, xc_ref, g_ref, w_in_ref, b_in_ref, caw_ref, cab_ref, lng_ref, lnb_ref,
            cbw_ref, w_out_ref, o_ref, ae_ref, ao_ref, atmp_ref, cv_ref, sext_ref, gb_ref, y_ref,
            h_ref)

    @pl.when(slot_a == 0)
    def _():
        _mixer_stages(0, 1, *refs)

    @pl.when(slot_a == 1)
    def _():
        _mixer_stages(1, 0, *refs)


def _mixer_stages(slot_a, slot_b, xa_ref, xc_ref, g_ref, w_in_ref, b_in_ref, caw_ref, cab_ref,
                  lng_ref, lnb_ref, cbw_ref, w_out_ref, o_ref, ae_ref, ao_ref, atmp_ref, cv_ref,
                  sext_ref, gb_ref, y_ref, h_ref):
    t = xa_ref.shape[1]
    d_conf = cab_ref.shape[1]
    n_slab = d_conf // LANES
    pack_rows = 2 * SUBLANES

    ae = ae_ref.at[slot_b]
    ao = ao_ref.at[slot_b]
    sext = sext_ref.at[slot_b]
    gb = gb_ref.at[slot_b]
    y = y_ref.at[slot_b]
    cslices = [slice(c * LANES, (c + 1) * LANES) for c in range(n_slab)]

    def in_proj(n):
        cols = slice(n * d_conf, (n + 1) * d_conf)
        return (jnp.dot(h_ref[...], w_in_ref[:, cols], preferred_element_type=jnp.float32)
                + b_in_ref[:, cols])

    def to_slabs(ref, halo, val):
        for c in range(n_slab):
            ref[c, halo:halo + t, :] = val[:, cslices[c]]

    def from_slabs(ref, halo):
        return jnp.concatenate([ref[c, halo:halo + t, :] for c in range(n_slab)], axis=1)

    def pack_pairs():
        for c in range(n_slab):
            for r in range(0, t, pack_rows):
                dst = pl.ds(HALO_PAIRS + r // 2, SUBLANES)
                even = atmp_ref[c, pl.ds(SUBLANES + r, pack_rows), :]
                odd = atmp_ref[c, pl.ds(SUBLANES + r - 1, pack_rows), :]
                ae_ref[slot_a, c, dst, :] = pltpu.bitcast(even.astype(jnp.bfloat16), jnp.uint32)
                ao_ref[slot_a, c, dst, :] = pltpu.bitcast(odd.astype(jnp.bfloat16), jnp.uint32)
            atmp_ref[c, 0:SUBLANES, :] = atmp_ref[c, t:t + SUBLANES, :]

    def conv31_tile():
        n_groups = 8
        span = n_groups * pack_rows

        def body(it, carry):
            pair0 = pl.multiple_of(it * (span // 2), span // 2)
            for c in range(n_slab):
                acc = [None] * n_groups
                for k in range(CONF_KERNEL):
                    wk = caw_ref[k, :, cslices[c]].astype(jnp.float32)
                    shift = CONF_KERNEL - 1 - k
                    src = ae if shift % 2 == 0 else ao
                    for i in range(n_groups):
                        pair = (i * pack_rows - shift + shift % 2) // 2
                        words = src[c, pl.ds(pair0 + HALO_PAIRS + pair, SUBLANES), :]
                        term = pltpu.bitcast(words, jnp.bfloat16).astype(jnp.float32) * wk
                        acc[i] = term if acc[i] is None else acc[i] + term
                bias = jnp.broadcast_to(cab_ref[:, cslices[c]], (pack_rows, LANES))
                for i in range(n_groups):
                    row = pl.multiple_of(it * span + i * pack_rows, pack_rows)
                    cv_ref[c, pl.ds(row, pack_rows), :] = acc[i] + bias
            return carry

        lax.fori_loop(0, t // span, body, 0)

    def conv_rows(r_start, r_stop):
        ln_g = [lng_ref[:, cs] for cs in cslices]
        ln_b = [lnb_ref[:, cs] for cs in cslices]
        short_w = [[cbw_ref[k, :, cs] for k in range(SHORT_KERNEL)] for cs in cslices]
        for r in range(r_start, r_stop, pack_rows):
            conv = [cv_ref[c, r:r + pack_rows, :] for c in range(n_slab)]
            tot = conv[0]
            for c in range(1, n_slab):
                tot = tot + conv[c]
            mu = jnp.sum(tot, axis=-1, keepdims=True) * (1.0 / d_conf)
            xc = [conv[c] - mu for c in range(n_slab)]
            sq = xc[0] * xc[0]
            for c in range(1, n_slab):
                sq = sq + xc[c] * xc[c]
            var = jnp.sum(sq, axis=-1, keepdims=True) * (1.0 / d_conf)
            inv = lax.rsqrt(var + EPS)
            for c in range(n_slab):
                v = xc[c] * inv * ln_g[c] + ln_b[c]
                y[r:r + pack_rows, cslices[c]] = (v * _sigmoid(v)).astype(jnp.bfloat16)
            for c in range(n_slab):
                outs = []
                for r8 in range(r, r + pack_rows, SUBLANES):
                    sacc = None
                    for k in range(SHORT_KERNEL):
                        off = HALO_S - (SHORT_KERNEL - 1) + k
                        term = sext[c, pl.ds(r8 + off, SUBLANES), :] * short_w[c][k]
                        sacc = term if sacc is None else sacc + term
                    outs.append(sacc * gb[r8:r8 + SUBLANES, cslices[c]])
                y[r:r + pack_rows, d_conf + c * LANES:d_conf + (c + 1) * LANES] = (
                    jnp.concatenate(outs, axis=0).astype(jnp.bfloat16))

    def out_proj(n, width):
        cols = slice(n * width, (n + 1) * width)
        o_ref[0, :, cols] = xc_ref[0, :, cols] + jnp.dot(
            y_ref[slot_a], w_out_ref[:, cols], preferred_element_type=jnp.float32)

    n_blocks = 8
    rows = t // n_blocks
    out_w = o_ref.shape[2] // 2
    conv31_tile()
    out_proj(0, out_w)
    h_ref[...] = _rmsnorm(xa_ref[0], g_ref[...]).astype(jnp.bfloat16)
    conv_rows(0 * rows, 1 * rows)
    out_proj(1, out_w)
    conv_rows(1 * rows, 2 * rows)
    to_slabs(atmp_ref, SUBLANES, _sigmoid(in_proj(1)))
    conv_rows(2 * rows, 3 * rows)
    to_slabs(atmp_ref, SUBLANES, in_proj(0) * from_slabs(atmp_ref, SUBLANES))
    pack_pairs()
    conv_rows(3 * rows, 4 * rows)
    to_slabs(sext_ref.at[slot_a], HALO_S, in_proj(3))
    conv_rows(4 * rows, 5 * rows)
    to_slabs(sext_ref.at[slot_a], HALO_S,
             in_proj(4) * from_slabs(sext_ref.at[slot_a], HALO_S))
    conv_rows(5 * rows, 6 * rows)
    gb_ref[slot_a] = in_proj(2)
    conv_rows(6 * rows, 7 * rows)
    conv_rows(7 * rows, 8 * rows)
    half_t = t // 2
    ae_ref[slot_a, :, 0:HALO_PAIRS, :] = ae_ref[slot_b, :, half_t:half_t + HALO_PAIRS, :]
    ao_ref[slot_a, :, 0:HALO_PAIRS, :] = ao_ref[slot_b, :, half_t:half_t + HALO_PAIRS, :]
    sext_ref[slot_a, :, 0:HALO_S, :] = sext_ref[slot_b, :, t:t + HALO_S, :]


def _ffn_kernel(x_ref, g_ref, w_up_ref, cfw_ref, w_down_ref, fg_ref, o_ref,
                h_ref, work_ref, halo_ref, act_ref, *, final_norm, tiles_per_seq):
    t = x_ref.shape[1]
    d_ff = w_down_ref.shape[0]
    n_chunks = d_ff // FFN_CHUNK
    slabs = 2 * FFN_CHUNK // LANES
    half = slabs // 2
    pack_rows = 2 * SUBLANES

    def up_cols(cc, s):
        return (s // half) * d_ff + cc * FFN_CHUNK + (s % half) * LANES

    @pl.when(lax.rem(pl.program_id(0), tiles_per_seq) == 0)
    def _():
        halo_ref[...] = jnp.zeros(halo_ref.shape, jnp.float32)

    h_ref[...] = _rmsnorm(x_ref[0], g_ref[...]).astype(jnp.bfloat16)

    def up(cc):
        for part in range(2):
            c0 = up_cols(cc, part * half)
            u = jnp.dot(h_ref[...], w_up_ref[:, c0:c0 + FFN_CHUNK],
                        preferred_element_type=jnp.float32)
            for s in range(half):
                slab = part * half + s
                work_ref[cc % 2, slab, 0:HALO_S, :] = halo_ref[cc * slabs + slab]
                work_ref[cc % 2, slab, HALO_S:HALO_S + t, :] = u[:, s * LANES:(s + 1) * LANES]

    def conv_gate(cc):
        work = work_ref.at[cc % 2]
        for s in range(half):
            cg = up_cols(cc, s)
            cv = up_cols(cc, half + s)
            wg = [cfw_ref[k, :, cg:cg + LANES] for k in range(SHORT_KERNEL)]
            wv = [cfw_ref[k, :, cv:cv + LANES] for k in range(SHORT_KERNEL)]
            for r0 in range(0, t, pack_rows):
                parts = []
                for r in range(r0, r0 + pack_rows, SUBLANES):
                    gate = None
                    val = None
                    for k in range(SHORT_KERNEL):
                        rows = pl.ds(r + HALO_S - (SHORT_KERNEL - 1) + k, SUBLANES)
                        tg = work[s, rows, :] * wg[k]
                        tv = work[half + s, rows, :] * wv[k]
                        gate = tg if gate is None else gate + tg
                        val = tv if val is None else val + tv
                    parts.append(gate * _sigmoid(gate) * val)
                col = cc * FFN_CHUNK + s * LANES
                act_ref[r0:r0 + pack_rows, col:col + LANES] = (
                    jnp.concatenate(parts, axis=0).astype(jnp.bfloat16))
        for s in range(slabs):
            halo_ref[cc * slabs + s] = work[s, t:t + HALO_S, :]

    up(0)
    for cc in range(n_chunks):
        if cc + 1 < n_chunks:
            up(cc + 1)
        conv_gate(cc)

    out = x_ref[0] + jnp.dot(act_ref[...], w_down_ref[...], preferred_element_type=jnp.float32)
    if final_norm:
        out = _rmsnorm(out, fg_ref[...])
    o_ref[0] = out


def _const_spec(shape):
    zeros = (0,) * len(shape)
    return pl.BlockSpec(shape, lambda s: zeros, pipeline_mode=pl.Buffered(1))


def _layer_spec(stacked, layer):
    _, rows, cols = stacked.shape
    return pl.BlockSpec((None, rows, cols), lambda s: (layer, 0, 0), pipeline_mode=pl.Buffered(1))


def _tile_specs(x, t, lag):
    bsz, seq, d = x.shape
    assert seq % t == 0 and d % LANES == 0, (x.shape, t)
    tiles_per_seq = seq // t
    n_tiles = bsz * tiles_per_seq

    def tile_index(q):
        q = jnp.clip(q, 0, n_tiles - 1)
        return (q // tiles_per_seq, lax.rem(q, tiles_per_seq), 0)

    lead = pl.BlockSpec((1, t, d), lambda s: tile_index(s))
    lagged = pl.BlockSpec((1, t, d), lambda s: tile_index(s - lag))
    return n_tiles + lag, tiles_per_seq, lead, lagged


def _mixer_call(x, layer, g, w_in, b_in, caw, cab, lng, lnb, cbw, w_out):
    d = x.shape[2]
    d_conf = cab.shape[1]
    assert d_conf % LANES == 0 and w_in.shape[1:] == (d, 5 * d_conf), (w_in.shape, d_conf)
    assert w_out.shape[1:] == (2 * d_conf, d) and caw.shape[0] == CONF_KERNEL, (w_out.shape, caw.shape)
    n_slab = d_conf // LANES
    t = MIXER_TILE
    n_steps, tiles_per_seq, lead, lagged = _tile_specs(x, t, lag=2)
    consts = [g, w_in, b_in, caw, cab, lng, lnb, cbw, w_out]
    specs = [_layer_spec(c, layer) if c is w_in or c is w_out else _const_spec(c.shape)
             for c in consts]
    return pl.pallas_call(
        functools.partial(_mixer_kernel, tiles_per_seq=tiles_per_seq),
        grid=(n_steps,),
        in_specs=[lead, lagged] + specs,
        out_specs=lagged,
        out_shape=jax.ShapeDtypeStruct(x.shape, x.dtype),
        scratch_shapes=[
            pltpu.VMEM((2, n_slab, HALO_PAIRS + t // 2, LANES), jnp.uint32),
            pltpu.VMEM((2, n_slab, HALO_PAIRS + t // 2, LANES), jnp.uint32),
            pltpu.VMEM((n_slab, SUBLANES + t, LANES), jnp.float32),
            pltpu.VMEM((n_slab, t, LANES), jnp.float32),
            pltpu.VMEM((2, n_slab, HALO_S + t, LANES), jnp.float32),
            pltpu.VMEM((2, t, d_conf), jnp.float32),
            pltpu.VMEM((2, t, 2 * d_conf), jnp.bfloat16),
            pltpu.VMEM((t, d), jnp.bfloat16),
        ],
        compiler_params=pltpu.CompilerParams(
            dimension_semantics=("arbitrary",),
            vmem_limit_bytes=VMEM_LIMIT_BYTES),
        name="mixer",
    )(x, x, *consts)


def _ffn_call(x, layer, g, w_up, cfw, w_down, fg, final_norm):
    d = x.shape[2]
    d_ff = w_down.shape[1]
    assert d_ff % FFN_CHUNK == 0 and w_up.shape[1:] == (d, 2 * d_ff), (w_up.shape, w_down.shape)
    slabs = 2 * FFN_CHUNK // LANES
    t = FFN_TILE
    n_steps, tiles_per_seq, lead, _ = _tile_specs(x, t, lag=0)
    consts = [g, w_up, cfw, w_down, fg]
    specs = [_layer_spec(c, layer) if c is w_up or c is w_down else _const_spec(c.shape)
             for c in consts]
    return pl.pallas_call(
        functools.partial(_ffn_kernel, final_norm=final_norm, tiles_per_seq=tiles_per_seq),
        grid=(n_steps,),
        in_specs=[lead] + specs,
        out_specs=lead,
        out_shape=jax.ShapeDtypeStruct(x.shape, x.dtype),
        scratch_shapes=[
            pltpu.VMEM((t, d), jnp.bfloat16),
            pltpu.VMEM((2, slabs, HALO_S + t, LANES), jnp.float32),
            pltpu.VMEM((d_ff // FFN_CHUNK * slabs, HALO_S, LANES), jnp.float32),
            pltpu.VMEM((t, d_ff), jnp.bfloat16),
        ],
        compiler_params=pltpu.CompilerParams(
            dimension_semantics=("arbitrary",),
            vmem_limit_bytes=VMEM_LIMIT_BYTES),
        name="ffn_final" if final_norm else "ffn",
    )(x, *consts)


def _sublane_bcast(w, rows=SUBLANES):
    k, c = w.shape
    return jnp.broadcast_to(w[:, None, :], (k, rows, c))


def kernel(x, mix_norm_g, w_in, b_in, conv_a_w, conv_a_b, ln_a_g, ln_a_b, conv_b_w, w_out,
           ffn_norm_g, w_up, conv_f_w, w_down, final_norm_g):
    depth = w_in.shape[0]
    bf16 = jnp.bfloat16
    w_in, w_out, w_up, w_down = (w.astype(bf16) for w in (w_in, w_out, w_up, w_down))
    for l in range(depth):
        x = _mixer_call(
            x, l, mix_norm_g[l][None], w_in, b_in[l][None],
            _sublane_bcast(conv_a_w[l].astype(bf16), 2 * SUBLANES), conv_a_b[l][None],
            ln_a_g[l][None], ln_a_b[l][None], _sublane_bcast(conv_b_w[l]), w_out)
        x = _ffn_call(x, l, ffn_norm_g[l][None], w_up, _sublane_bcast(conv_f_w[l]), w_down,
                      final_norm_g[None], final_norm=(l == depth - 1))
    return x
```

```python
import functools

import jax
import jax.numpy as jnp
from jax import lax
from jax.experimental import pallas as pl
from jax.experimental.pallas import tpu as pltpu

EPS = 1e-6
LANES = 128
SUBLANES = 8

CONF_KERNEL = 31
SHORT_KERNEL = 3
FFN_CHUNK = 256

MIXER_TILE = 512
FFN_TILE = 512
HALO_PAIRS = 16
HALO_S = SUBLANES
VMEM_LIMIT_BYTES = 56 * 1024 * 1024


def _rmsnorm(x, g):
    ms = jnp.mean(x * x, axis=-1, keepdims=True)
    return x * lax.rsqrt(ms + EPS) * g


def _sigmoid(x):
    return 1.0 / (1.0 + jnp.exp(-x))


def _mixer_kernel(xa_ref, xc_ref, g_ref, w_in_ref, b_in_ref, caw_ref, cab_ref, lng_ref, lnb_ref,
                  cbw_ref, w_out_ref, o_ref, ae_ref, ao_ref, atmp_ref, cv_ref, sext_ref, gb_ref, y_ref,
                  h_ref, *, tiles_per_seq):
    n_slab = cab_ref.shape[1] // LANES
    step = pl.program_id(0)
    slot_a = lax.rem(step, 2)
    slot_b = 1 - slot_a

    @pl.when(step == 0)
    def _():
        for ref in (ae_ref, ao_ref, atmp_ref, cv_ref, sext_ref, gb_ref, y_ref):
            ref[...] = jnp.zeros(ref.shape, ref.dtype)

    @pl.when(lax.rem(step + tiles_per_seq - 1, tiles_per_seq) == 0)
    def _():
        ae_ref[slot_b, :, 0:HALO_PAIRS, :] = jnp.zeros((n_slab, HALO_PAIRS, LANES), jnp.uint32)
        ao_ref[slot_b, :, 0:HALO_PAIRS, :] = jnp.zeros((n_slab, HALO_PAIRS, LANES), jnp.uint32)
        sext_ref[slot_b, :, 0:HALO_S, :] = jnp.zeros((n_slab, HALO_S, LANES), jnp.float32)

    @pl.when(lax.rem(step, tiles_per_seq) == 0)
    def _():
        atmp_ref[:, 0:SUBLANES, :] = jnp.zeros((n_slab, SUBLANES, LANES), jnp.float32)

    refs = (xa_ref, xc_ref, g_ref, w_in_ref, b_in_ref, caw_ref, cab_ref, lng_ref, lnb_ref,
            cbw_ref, w_out_ref, o_ref, ae_ref, ao_ref, atmp_ref, cv_ref, sext_ref, gb_ref, y_ref,
            h_ref)

    @pl.when(slot_a == 0)
    def _():
        _mixer_stages(0, 1, *refs)

    @pl.when(slot_a == 1)
    def _():
        _mixer_stages(1, 0, *refs)


def _mixer_stages(slot_a, slot_b, xa_ref, xc_ref, g_ref, w_in_ref, b_in_ref, caw_ref, cab_ref,
                  lng_ref, lnb_ref, cbw_ref, w_out_ref, o_ref, ae_ref, ao_ref, atmp_ref, cv_ref,
                  sext_ref, gb_ref, y_ref, h_ref):
    t = xa_ref.shape[1]
    d_conf = cab_ref.shape[1]
    n_slab = d_conf // LANES
    pack_rows = 2 * SUBLANES

    ae = ae_ref.at[slot_b]
    ao = ao_ref.at[slot_b]
    sext = sext_ref.at[slot_b]
    gb = gb_ref.at[slot_b]
    y = y_ref.at[slot_b]
    cslices = [slice(c * LANES, (c + 1) * LANES) for c in range(n_slab)]

    def in_proj(n):
        cols = slice(n * d_conf, (n + 1) * d_conf)
        return (jnp.dot(h_ref[...], w_in_ref[:, cols], preferred_element_type=jnp.float32)
                + b_in_ref[:, cols])

    def to_slabs(ref, halo, val):
        for c in range(n_slab):
            ref[c, halo:halo + t, :] = val[:, cslices[c]]

    def from_slabs(ref, halo):
        return jnp.concatenate([ref[c, halo:halo + t, :] for c in range(n_slab)], axis=1)

    def pack_pairs():
        for c in range(n_slab):
            for r in range(0, t, pack_rows):
                dst = pl.ds(HALO_PAIRS + r // 2, SUBLANES)
                even = atmp_ref[c, pl.ds(SUBLANES + r, pack_rows), :]
                odd = atmp_ref[c, pl.ds(SUBLANES + r - 1, pack_rows), :]
                ae_ref[slot_a, c, dst, :] = pltpu.bitcast(even.astype(jnp.bfloat16), jnp.uint32)
                ao_ref[slot_a, c, dst, :] = pltpu.bitcast(odd.astype(jnp.bfloat16), jnp.uint32)
            atmp_ref[c, 0:SUBLANES, :] = atmp_ref[c, t:t + SUBLANES, :]

    def conv31_tile():
        n_groups = 16
        span = n_groups * pack_rows

        def body(it, carry):
            pair0 = pl.multiple_of(it * (span // 2), span // 2)
            for c in range(n_slab):
                acc = [None] * n_groups
                for k in range(CONF_KERNEL):
                    wk = caw_ref[k, :, cslices[c]].astype(jnp.float32)
                    shift = CONF_KERNEL - 1 - k
                    src = ae if shift % 2 == 0 else ao
                    for i in range(n_groups):
                        pair = (i * pack_rows - shift + shift % 2) // 2
                        words = src[c, pl.ds(pair0 + HALO_PAIRS + pair, SUBLANES), :]
                        term = pltpu.bitcast(words, jnp.bfloat16).astype(jnp.float32) * wk
                        acc[i] = term if acc[i] is None else acc[i] + term
                bias = jnp.broadcast_to(cab_ref[:, cslices[c]], (pack_rows, LANES))
                for i in range(n_groups):
                    row = pl.multiple_of(it * span + i * pack_rows, pack_rows)
                    cv_ref[c, pl.ds(row, pack_rows), :] = acc[i] + bias
            return carry

        lax.fori_loop(0, t // span, body, 0)

    def conv_rows(r_start, r_stop):
        ln_g = [lng_ref[:, cs] for cs in cslices]
        ln_b = [lnb_ref[:, cs] for cs in cslices]
        short_w = [[cbw_ref[k, :, cs] for k in range(SHORT_KERNEL)] for cs in cslices]
        for r in range(r_start, r_stop, pack_rows):
            conv = [cv_ref[c, r:r + pack_rows, :] for c in range(n_slab)]
            tot = conv[0]
            for c in range(1, n_slab):
                tot = tot + conv[c]
            mu = jnp.sum(tot, axis=-1, keepdims=True) * (1.0 / d_conf)
            xc = [conv[c] - mu for c in range(n_slab)]
            sq = xc[0] * xc[0]
            for c in range(1, n_slab):
                sq = sq + xc[c] * xc[c]
            var = jnp.sum(sq, axis=-1, keepdims=True) * (1.0 / d_conf)
            inv = lax.rsqrt(var + EPS)
            for c in range(n_slab):
                v = xc[c] * inv * ln_g[c] + ln_b[c]
                y[r:r + pack_rows, cslices[c]] = (v * _sigmoid(v)).astype(jnp.bfloat16)
            for c in range(n_slab):
                outs = []
                for r8 in range(r, r + pack_rows, SUBLANES):
                    sacc = None
                    for k in range(SHORT_KERNEL):
                        off = HALO_S - (SHORT_KERNEL - 1) + k
                        term = sext[c, pl.ds(r8 + off, SUBLANES), :] * short_w[c][k]
                        sacc = term if sacc is None else sacc + term
                    outs.append(sacc * gb[r8:r8 + SUBLANES, cslices[c]])
                y[r:r + pack_rows, d_conf + c * LANES:d_conf + (c + 1) * LANES] = (
                    jnp.concatenate(outs, axis=0).astype(jnp.bfloat16))

    def out_proj(n, width):
        cols = slice(n * width, (n + 1) * width)
        o_ref[0, :, cols] = xc_ref[0, :, cols] + jnp.dot(
            y_ref[slot_a], w_out_ref[:, cols], preferred_element_type=jnp.float32)

    n_blocks = 8
    rows = t // n_blocks
    out_w = o_ref.shape[2] // 2
    conv31_tile()
    out_proj(0, out_w)
    h_ref[...] = _rmsnorm(xa_ref[0], g_ref[...]).astype(jnp.bfloat16)
    conv_rows(0 * rows, 1 * rows)
    out_proj(1, out_w)
    conv_rows(1 * rows, 2 * rows)
    to_slabs(atmp_ref, SUBLANES, _sigmoid(in_proj(1)))
    conv_rows(2 * rows, 3 * rows)
    to_slabs(atmp_ref, SUBLANES, in_proj(0) * from_slabs(atmp_ref, SUBLANES))
    pack_pairs()
    conv_rows(3 * rows, 4 * rows)
    to_slabs(sext_ref.at[slot_a], HALO_S, in_proj(3))
    conv_rows(4 * rows, 5 * rows)
    to_slabs(sext_ref.at[slot_a], HALO_S,
             in_proj(4) * from_slabs(sext_ref.at[slot_a], HALO_S))
    conv_rows(5 * rows, 6 * rows)
    gb_ref[slot_a] = in_proj(2)
    conv_rows(6 * rows, 7 * rows)
    conv_rows(7 * rows, 8 * rows)
    half_t = t // 2
    ae_ref[slot_a, :, 0:HALO_PAIRS, :] = ae_ref[slot_b, :, half_t:half_t + HALO_PAIRS, :]
    ao_ref[slot_a, :, 0:HALO_PAIRS, :] = ao_ref[slot_b, :, half_t:half_t + HALO_PAIRS, :]
    sext_ref[slot_a, :, 0:HALO_S, :] = sext_ref[slot_b, :, t:t + HALO_S, :]


def _ffn_kernel(x_ref, g_ref, w_up_ref, cfw_ref, w_down_ref, fg_ref, o_ref,
                h_ref, work_ref, halo_ref, act_ref, *, final_norm, tiles_per_seq):
    t = x_ref.shape[1]
    d_ff = w_down_ref.shape[0]
    n_chunks = d_ff // FFN_CHUNK
    slabs = 2 * FFN_CHUNK // LANES
    half = slabs // 2
    pack_rows = 2 * SUBLANES

    def up_cols(cc, s):
        return (s // half) * d_ff + cc * FFN_CHUNK + (s % half) * LANES

    @pl.when(lax.rem(pl.program_id(0), tiles_per_seq) == 0)
    def _():
        halo_ref[...] = jnp.zeros(halo_ref.shape, jnp.float32)

    h_ref[...] = _rmsnorm(x_ref[0], g_ref[...]).astype(jnp.bfloat16)

    def up(cc):
        for part in range(2):
            c0 = up_cols(cc, part * half)
            u = jnp.dot(h_ref[...], w_up_ref[:, c0:c0 + FFN_CHUNK],
                        preferred_element_type=jnp.float32)
            for s in range(half):
                slab = part * half + s
                work_ref[cc % 2, slab, 0:HALO_S, :] = halo_ref[cc * slabs + slab]
                work_ref[cc % 2, slab, HALO_S:HALO_S + t, :] = u[:, s * LANES:(s + 1) * LANES]

    def conv_gate(cc):
        work = work_ref.at[cc % 2]
        for s in range(half):
            cg = up_cols(cc, s)
            cv = up_cols(cc, half + s)
            wg = [cfw_ref[k, :, cg:cg + LANES] for k in range(SHORT_KERNEL)]
            wv = [cfw_ref[k, :, cv:cv + LANES] for k in range(SHORT_KERNEL)]
            for r0 in range(0, t, pack_rows):
                parts = []
                for r in range(r0, r0 + pack_rows, SUBLANES):
                    gate = None
                    val = None
                    for k in range(SHORT_KERNEL):
                        rows = pl.ds(r + HALO_S - (SHORT_KERNEL - 1) + k, SUBLANES)
                        tg = work[s, rows, :] * wg[k]
                        tv = work[half + s, rows, :] * wv[k]
                        gate = tg if gate is None else gate + tg
                        val = tv if val is None else val + tv
                    parts.append(gate * _sigmoid(gate) * val)
                col = cc * FFN_CHUNK + s * LANES
                act_ref[r0:r0 + pack_rows, col:col + LANES] = (
                    jnp.concatenate(parts, axis=0).astype(jnp.bfloat16))
        for s in range(slabs):
            halo_ref[cc * slabs + s] = work[s, t:t + HALO_S, :]

    up(0)
    for cc in range(n_chunks):
        if cc + 1 < n_chunks:
            up(cc + 1)
        conv_gate(cc)

    out = x_ref[0] + jnp.dot(act_ref[...], w_down_ref[...], preferred_element_type=jnp.float32)
    if final_norm:
        out = _rmsnorm(out, fg_ref[...])
    o_ref[0] = out


def _const_spec(shape):
    zeros = (0,) * len(shape)
    return pl.BlockSpec(shape, lambda s: zeros, pipeline_mode=pl.Buffered(1))


def _layer_spec(stacked, layer):
    _, rows, cols = stacked.shape
    return pl.BlockSpec((None, rows, cols), lambda s: (layer, 0, 0), pipeline_mode=pl.Buffered(1))


def _tile_specs(x, t, lag):
    bsz, seq, d = x.shape
    assert seq % t == 0 and d % LANES == 0, (x.shape, t)
    tiles_per_seq = seq // t
    n_tiles = bsz * tiles_per_seq

    def tile_index(q):
        q = jnp.clip(q, 0, n_tiles - 1)
        return (q // tiles_per_seq, lax.rem(q, tiles_per_seq), 0)

    lead = pl.BlockSpec((1, t, d), lambda s: tile_index(s))
    lagged = pl.BlockSpec((1, t, d), lambda s: tile_index(s - lag))
    return n_tiles + lag, tiles_per_seq, lead, lagged


def _mixer_call(x, layer, g, w_in, b_in, caw, cab, lng, lnb, cbw, w_out):
    d = x.shape[2]
    d_conf = cab.shape[1]
    assert d_conf % LANES == 0 and w_in.shape[1:] == (d, 5 * d_conf), (w_in.shape, d_conf)
    assert w_out.shape[1:] == (2 * d_conf, d) and caw.shape[0] == CONF_KERNEL, (w_out.shape, caw.shape)
    n_slab = d_conf // LANES
    t = MIXER_TILE
    n_steps, tiles_per_seq, lead, lagged = _tile_specs(x, t, lag=2)
    consts = [g, w_in, b_in, caw, cab, lng, lnb, cbw, w_out]
    specs = [_layer_spec(c, layer) if c is w_in or c is w_out else _const_spec(c.shape)
             for c in consts]
    return pl.pallas_call(
        functools.partial(_mixer_kernel, tiles_per_seq=tiles_per_seq),
        grid=(n_steps,),
        in_specs=[lead, lagged] + specs,
        out_specs=lagged,
        out_shape=jax.ShapeDtypeStruct(x.shape, x.dtype),
        scratch_shapes=[
            pltpu.VMEM((2, n_slab, HALO_PAIRS + t // 2, LANES), jnp.uint32),
            pltpu.VMEM((2, n_slab, HALO_PAIRS + t // 2, LANES), jnp.uint32),
            pltpu.VMEM((n_slab, SUBLANES + t, LANES), jnp.float32),
            pltpu.VMEM((n_slab, t, LANES), jnp.float32),
            pltpu.VMEM((2, n_slab, HALO_S + t, LANES), jnp.float32),
            pltpu.VMEM((2, t, d_conf), jnp.float32),
            pltpu.VMEM((2, t, 2 * d_conf), jnp.bfloat16),
            pltpu.VMEM((t, d), jnp.bfloat16),
        ],
        compiler_params=pltpu.CompilerParams(
            dimension_semantics=("arbitrary",),
            vmem_limit_bytes=VMEM_LIMIT_BYTES),
        name="mixer",
    )(x, x, *consts)


def _ffn_call(x, layer, g, w_up, cfw, w_down, fg, final_norm):
    d = x.shape[2]
    d_ff = w_down.shape[1]
    assert d_ff % FFN_CHUNK == 0 and w_up.shape[1:] == (d, 2 * d_ff), (w_up.shape, w_down.shape)
    slabs = 2 * FFN_CHUNK // LANES
    t = FFN_TILE
    n_steps, tiles_per_seq, lead, _ = _tile_specs(x, t, lag=0)
    consts = [g, w_up, cfw, w_down, fg]
    specs = [_layer_spec(c, layer) if c is w_up or c is w_down else _const_spec(c.shape)
             for c in consts]
    return pl.pallas_call(
        functools.partial(_ffn_kernel, final_norm=final_norm, tiles_per_seq=tiles_per_seq),
        grid=(n_steps,),
        in_specs=[lead] + specs,
        out_specs=lead,
        out_shape=jax.ShapeDtypeStruct(x.shape, x.dtype),
        scratch_shapes=[
            pltpu.VMEM((t, d), jnp.bfloat16),
            pltpu.VMEM((2, slabs, HALO_S + t, LANES), jnp.float32),
            pltpu.VMEM((d_ff // FFN_CHUNK * slabs, HALO_S, LANES), jnp.float32),
            pltpu.VMEM((t, d_ff), jnp.bfloat16),
        ],
        compiler_params=pltpu.CompilerParams(
            dimension_semantics=("arbitrary",),
            vmem_limit_bytes=VMEM_LIMIT_BYTES),
        name="ffn_final" if final_norm else "ffn",
    )(x, *consts)


def _sublane_bcast(w, rows=SUBLANES):
    k, c = w.shape
    return jnp.broadcast_to(w[:, None, :], (k, rows, c))


def kernel(x, mix_norm_g, w_in, b_in, conv_a_w, conv_a_b, ln_a_g, ln_a_b, conv_b_w, w_out,
           ffn_norm_g, w_up, conv_f_w, w_down, final_norm_g):
    depth = w_in.shape[0]
    bf16 = jnp.bfloat16
    w_in, w_out, w_up, w_down = (w.astype(bf16) for w in (w_in, w_out, w_up, w_down))
    for l in range(depth):
        x = _mixer_call(
            x, l, mix_norm_g[l][None], w_in, b_in[l][None],
            _sublane_bcast(conv_a_w[l].astype(bf16), 2 * SUBLANES), conv_a_b[l][None],
            ln_a_g[l][None], ln_a_b[l][None], _sublane_bcast(conv_b_w[l]), w_out)
        x = _ffn_call(x, l, ffn_norm_g[l][None], w_up, _sublane_bcast(conv_f_w[l]), w_down,
                      final_norm_g[None], final_norm=(l == depth - 1))
    return x
```

```python
import functools

import jax
import jax.numpy as jnp
from jax import lax
from jax.experimental import pallas as pl
from jax.experimental.pallas import tpu as pltpu

EPS = 1e-6
LANES = 128
SUBLANES = 8

CONF_KERNEL = 31
SHORT_KERNEL = 3
FFN_CHUNK = 256

MIXER_TILE = 512
FFN_TILE = 512
HALO_PAIRS = 16
HALO_S = SUBLANES
VMEM_LIMIT_BYTES = 56 * 1024 * 1024


def _rmsnorm(x, g):
    ms = jnp.mean(x * x, axis=-1, keepdims=True)
    return x * lax.rsqrt(ms + EPS) * g


def _sigmoid(x):
    return 1.0 / (1.0 + jnp.exp(-x))


def _mixer_kernel(xa_ref, xc_ref, g_ref, w_in_ref, b_in_ref, caw_ref, cab_ref, lng_ref, lnb_ref,
                  cbw_ref, w_out_ref, o_ref, ae_ref, ao_ref, atmp_ref, cv_ref, sext_ref, gb_ref, y_ref,
                  h_ref, *, tiles_per_seq):
    n_slab = cab_ref.shape[1] // LANES
    step = pl.program_id(0)
    slot_a = lax.rem(step, 2)
    slot_b = 1 - slot_a

    @pl.when(step == 0)
    def _():
        for ref in (ae_ref, ao_ref, atmp_ref, cv_ref, sext_ref, gb_ref, y_ref):
            ref[...] = jnp.zeros(ref.shape, ref.dtype)

    @pl.when(lax.rem(step + tiles_per_seq - 1, tiles_per_seq) == 0)
    def _():
        ae_ref[slot_b, :, 0:HALO_PAIRS, :] = jnp.zeros((n_slab, HALO_PAIRS, LANES), jnp.uint32)
        ao_ref[slot_b, :, 0:HALO_PAIRS, :] = jnp.zeros((n_slab, HALO_PAIRS, LANES), jnp.uint32)
        sext_ref[slot_b, :, 0:HALO_S, :] = jnp.zeros((n_slab, HALO_S, LANES), jnp.float32)

    @pl.when(lax.rem(step, tiles_per_seq) == 0)
    def _():
        atmp_ref[:, 0:SUBLANES, :] = jnp.zeros((n_slab, SUBLANES, LANES), jnp.float32)

    refs = (xa_ref, xc_ref, g_ref, w_in_ref, b_in_ref, caw_ref, cab_ref, lng_ref, lnb_ref,
            cbw_ref, w_out_ref, o_ref, ae_ref, ao_ref, atmp_ref, cv_ref, sext_ref, gb_ref, y_ref,
            h_ref)

    @pl.when(slot_a == 0)
    def _():
        _mixer_stages(0, 1, *refs)

    @pl.when(slot_a == 1)
    def _():
        _mixer_stages(1, 0, *refs)


def _mixer_stages(slot_a, slot_b, xa_ref, xc_ref, g_ref, w_in_ref, b_in_ref, caw_ref, cab_ref,
                  lng_ref, lnb_ref, cbw_ref, w_out_ref, o_ref, ae_ref, ao_ref, atmp_ref, cv_ref,
                  sext_ref, gb_ref, y_ref, h_ref):
    t = xa_ref.shape[1]
    d_conf = cab_ref.shape[1]
    n_slab = d_conf // LANES
    pack_rows = 2 * SUBLANES

    ae = ae_ref.at[slot_b]
    ao = ao_ref.at[slot_b]
    sext = sext_ref.at[slot_b]
    gb = gb_ref.at[slot_b]
    y = y_ref.at[slot_b]
    cslices = [slice(c * LANES, (c + 1) * LANES) for c in range(n_slab)]

    def in_proj(n):
        cols = slice(n * d_conf, (n + 1) * d_conf)
        return (jnp.dot(h_ref[...], w_in_ref[:, cols], preferred_element_type=jnp.float32)
                + b_in_ref[:, cols])

    def to_slabs(ref, halo, val):
        for c in range(n_slab):
            ref[c, halo:halo + t, :] = val[:, cslices[c]]

    def from_slabs(ref, halo):
        return jnp.concatenate([ref[c, halo:halo + t, :] for c in range(n_slab)], axis=1)

    def pack_pairs():
        for c in range(n_slab):
            for r in range(0, t, pack_rows):
                dst = pl.ds(HALO_PAIRS + r // 2, SUBLANES)
                even = atmp_ref[c, pl.ds(SUBLANES + r, pack_rows), :]
                odd = atmp_ref[c, pl.ds(SUBLANES + r - 1, pack_rows), :]
                ae_ref[slot_a, c, dst, :] = pltpu.bitcast(even.astype(jnp.bfloat16), jnp.uint32)
                ao_ref[slot_a, c, dst, :] = pltpu.bitcast(odd.astype(jnp.bfloat16), jnp.uint32)
            atmp_ref[c, 0:SUBLANES, :] = atmp_ref[c, t:t + SUBLANES, :]

    def conv31_tile():
        n_groups = 16
        span = n_groups * pack_rows

        def body(it, carry):
            pair0 = pl.multiple_of(it * (span // 2), span // 2)
            for c in range(n_slab):
                acc = [None] * n_groups
                for k in range(CONF_KERNEL):
                    wk = caw_ref[k, :, cslices[c]].astype(jnp.float32)
                    shift = CONF_KERNEL - 1 - k
                    src = ae if shift % 2 == 0 else ao
                    for i in range(n_groups):
                        pair = (i * pack_rows - shift + shift % 2) // 2
                        words = src[c, pl.ds(pair0 + HALO_PAIRS + pair, SUBLANES), :]
                        term = pltpu.bitcast(words, jnp.bfloat16).astype(jnp.float32) * wk
                        acc[i] = term if acc[i] is None else acc[i] + term
                bias = jnp.broadcast_to(cab_ref[:, cslices[c]], (pack_rows, LANES))
                for i in range(n_groups):
                    row = pl.multiple_of(it * span + i * pack_rows, pack_rows)
                    cv_ref[c, pl.ds(row, pack_rows), :] = acc[i] + bias
            return carry

        lax.fori_loop(0, t // span, body, 0)

    def rows_tile():
        span = 16 * pack_rows

        def body(it, carry):
            base = pl.multiple_of(it * span, span)
            ln_g = [lng_ref[:, cs] for cs in cslices]
            ln_b = [lnb_ref[:, cs] for cs in cslices]
            short_w = [[cbw_ref[k, :, cs] for k in range(SHORT_KERNEL)] for cs in cslices]
            for r in range(0, span, pack_rows):
                rows = pl.ds(pl.multiple_of(base + r, pack_rows), pack_rows)
                conv = [cv_ref[c, rows, :] for c in range(n_slab)]
                tot = conv[0]
                for c in range(1, n_slab):
                    tot = tot + conv[c]
                mu = jnp.sum(tot, axis=-1, keepdims=True) * (1.0 / d_conf)
                xc = [conv[c] - mu for c in range(n_slab)]
                sq = xc[0] * xc[0]
                for c in range(1, n_slab):
                    sq = sq + xc[c] * xc[c]
                var = jnp.sum(sq, axis=-1, keepdims=True) * (1.0 / d_conf)
                inv = lax.rsqrt(var + EPS)
                for c in range(n_slab):
                    v = xc[c] * inv * ln_g[c] + ln_b[c]
                    y[rows, cslices[c]] = (v * _sigmoid(v)).astype(jnp.bfloat16)
                for c in range(n_slab):
                    outs = []
                    for r8 in range(r, r + pack_rows, SUBLANES):
                        sacc = None
                        for k in range(SHORT_KERNEL):
                            off = HALO_S - (SHORT_KERNEL - 1) + k
                            term = sext[c, pl.ds(base + r8 + off, SUBLANES), :] * short_w[c][k]
                            sacc = term if sacc is None else sacc + term
                        gate_rows = pl.ds(pl.multiple_of(base + r8, SUBLANES), SUBLANES)
                        outs.append(sacc * gb[gate_rows, cslices[c]])
                    y[rows, d_conf + c * LANES:d_conf + (c + 1) * LANES] = (
                        jnp.concatenate(outs, axis=0).astype(jnp.bfloat16))
            return carry

        lax.fori_loop(0, t // span, body, 0)

    def out_proj(n, width):
        cols = slice(n * width, (n + 1) * width)
        o_ref[0, :, cols] = xc_ref[0, :, cols] + jnp.dot(
            y_ref[slot_a], w_out_ref[:, cols], preferred_element_type=jnp.float32)

    out_w = o_ref.shape[2] // 2
    conv31_tile()
    rows_tile()
    out_proj(0, out_w)
    h_ref[...] = _rmsnorm(xa_ref[0], g_ref[...]).astype(jnp.bfloat16)
    out_proj(1, out_w)
    to_slabs(atmp_ref, SUBLANES, _sigmoid(in_proj(1)))
    to_slabs(atmp_ref, SUBLANES, in_proj(0) * from_slabs(atmp_ref, SUBLANES))
    pack_pairs()
    to_slabs(sext_ref.at[slot_a], HALO_S, in_proj(3))
    to_slabs(sext_ref.at[slot_a], HALO_S,
             in_proj(4) * from_slabs(sext_ref.at[slot_a], HALO_S))
    gb_ref[slot_a] = in_proj(2)
    half_t = t // 2
    ae_ref[slot_a, :, 0:HALO_PAIRS, :] = ae_ref[slot_b, :, half_t:half_t + HALO_PAIRS, :]
    ao_ref[slot_a, :, 0:HALO_PAIRS, :] = ao_ref[slot_b, :, half_t:half_t + HALO_PAIRS, :]
    sext_ref[slot_a, :, 0:HALO_S, :] = sext_ref[slot_b, :, t:t + HALO_S, :]


def _ffn_kernel(x_ref, g_ref, w_up_ref, cfw_ref, w_down_ref, fg_ref, o_ref,
                h_ref, work_ref, halo_ref, act_ref, *, final_norm, tiles_per_seq):
    t = x_ref.shape[1]
    d_ff = w_down_ref.shape[0]
    n_chunks = d_ff // FFN_CHUNK
    slabs = 2 * FFN_CHUNK // LANES
    half = slabs // 2
    pack_rows = 2 * SUBLANES

    def up_cols(cc, s):
        return (s // half) * d_ff + cc * FFN_CHUNK + (s % half) * LANES

    @pl.when(lax.rem(pl.program_id(0), tiles_per_seq) == 0)
    def _():
        halo_ref[...] = jnp.zeros(halo_ref.shape, jnp.float32)

    h_ref[...] = _rmsnorm(x_ref[0], g_ref[...]).astype(jnp.bfloat16)

    def up(cc):
        for part in range(2):
            c0 = up_cols(cc, part * half)
            u = jnp.dot(h_ref[...], w_up_ref[:, c0:c0 + FFN_CHUNK],
                        preferred_element_type=jnp.float32)
            for s in range(half):
                slab = part * half + s
                work_ref[cc % 2, slab, 0:HALO_S, :] = halo_ref[cc * slabs + slab]
                work_ref[cc % 2, slab, HALO_S:HALO_S + t, :] = u[:, s * LANES:(s + 1) * LANES]

    def conv_gate(cc):
        work = work_ref.at[cc % 2]
        for s in range(half):
            cg = up_cols(cc, s)
            cv = up_cols(cc, half + s)
            wg = [cfw_ref[k, :, cg:cg + LANES] for k in range(SHORT_KERNEL)]
            wv = [cfw_ref[k, :, cv:cv + LANES] for k in range(SHORT_KERNEL)]
            for r0 in range(0, t, pack_rows):
                parts = []
                for r in range(r0, r0 + pack_rows, SUBLANES):
                    gate = None
                    val = None
                    for k in range(SHORT_KERNEL):
                        rows = pl.ds(r + HALO_S - (SHORT_KERNEL - 1) + k, SUBLANES)
                        tg = work[s, rows, :] * wg[k]
                        tv = work[half + s, rows, :] * wv[k]
                        gate = tg if gate is None else gate + tg
                        val = tv if val is None else val + tv
                    parts.append(gate * _sigmoid(gate) * val)
                col = cc * FFN_CHUNK + s * LANES
                act_ref[r0:r0 + pack_rows, col:col + LANES] = (
                    jnp.concatenate(parts, axis=0).astype(jnp.bfloat16))
        for s in range(slabs):
            halo_ref[cc * slabs + s] = work[s, t:t + HALO_S, :]

    up(0)
    for cc in range(n_chunks):
        if cc + 1 < n_chunks:
            up(cc + 1)
        conv_gate(cc)

    out = x_ref[0] + jnp.dot(act_ref[...], w_down_ref[...], preferred_element_type=jnp.float32)
    if final_norm:
        out = _rmsnorm(out, fg_ref[...])
    o_ref[0] = out


def _const_spec(shape):
    zeros = (0,) * len(shape)
    return pl.BlockSpec(shape, lambda s: zeros, pipeline_mode=pl.Buffered(1))


def _layer_spec(stacked, layer):
    _, rows, cols = stacked.shape
    return pl.BlockSpec((None, rows, cols), lambda s: (layer, 0, 0), pipeline_mode=pl.Buffered(1))


def _tile_specs(x, t, lag):
    bsz, seq, d = x.shape
    assert seq % t == 0 and d % LANES == 0, (x.shape, t)
    tiles_per_seq = seq // t
    n_tiles = bsz * tiles_per_seq

    def tile_index(q):
        q = jnp.clip(q, 0, n_tiles - 1)
        return (q // tiles_per_seq, lax.rem(q, tiles_per_seq), 0)

    lead = pl.BlockSpec((1, t, d), lambda s: tile_index(s))
    lagged = pl.BlockSpec((1, t, d), lambda s: tile_index(s - lag))
    return n_tiles + lag, tiles_per_seq, lead, lagged


def _mixer_call(x, layer, g, w_in, b_in, caw, cab, lng, lnb, cbw, w_out):
    d = x.shape[2]
    d_conf = cab.shape[1]
    assert d_conf % LANES == 0 and w_in.shape[1:] == (d, 5 * d_conf), (w_in.shape, d_conf)
    assert w_out.shape[1:] == (2 * d_conf, d) and caw.shape[0] == CONF_KERNEL, (w_out.shape, caw.shape)
    n_slab = d_conf // LANES
    t = MIXER_TILE
    n_steps, tiles_per_seq, lead, lagged = _tile_specs(x, t, lag=2)
    consts = [g, w_in, b_in, caw, cab, lng, lnb, cbw, w_out]
    specs = [_layer_spec(c, layer) if c is w_in or c is w_out else _const_spec(c.shape)
             for c in consts]
    return pl.pallas_call(
        functools.partial(_mixer_kernel, tiles_per_seq=tiles_per_seq),
        grid=(n_steps,),
        in_specs=[lead, lagged] + specs,
        out_specs=lagged,
        out_shape=jax.ShapeDtypeStruct(x.shape, x.dtype),
        scratch_shapes=[
            pltpu.VMEM((2, n_slab, HALO_PAIRS + t // 2, LANES), jnp.uint32),
            pltpu.VMEM((2, n_slab, HALO_PAIRS + t // 2, LANES), jnp.uint32),
            pltpu.VMEM((n_slab, SUBLANES + t, LANES), jnp.float32),
            pltpu.VMEM((n_slab, t, LANES), jnp.float32),
            pltpu.VMEM((2, n_slab, HALO_S + t, LANES), jnp.float32),
            pltpu.VMEM((2, t, d_conf), jnp.float32),
            pltpu.VMEM((2, t, 2 * d_conf), jnp.bfloat16),
            pltpu.VMEM((t, d), jnp.bfloat16),
        ],
        compiler_params=pltpu.CompilerParams(
            dimension_semantics=("arbitrary",),
            vmem_limit_bytes=VMEM_LIMIT_BYTES),
        name="mixer",
    )(x, x, *consts)


def _ffn_call(x, layer, g, w_up, cfw, w_down, fg, final_norm):
    d = x.shape[2]
    d_ff = w_down.shape[1]
    assert d_ff % FFN_CHUNK == 0 and w_up.shape[1:] == (d, 2 * d_ff), (w_up.shape, w_down.shape)
    slabs = 2 * FFN_CHUNK // LANES
    t = FFN_TILE
    n_steps, tiles_per_seq, lead, _ = _tile_specs(x, t, lag=0)
    consts = [g, w_up, cfw, w_down, fg]
    specs = [_layer_spec(c, layer) if c is w_up or c is w_down else _const_spec(c.shape)
             for c in consts]
    return pl.pallas_call(
        functools.partial(_ffn_kernel, final_norm=final_norm, tiles_per_seq=tiles_per_seq),
        grid=(n_steps,),
        in_specs=[lead] + specs,
        out_specs=lead,
        out_shape=jax.ShapeDtypeStruct(x.shape, x.dtype),
        scratch_shapes=[
            pltpu.VMEM((t, d), jnp.bfloat16),
            pltpu.VMEM((2, slabs, HALO_S + t, LANES), jnp.float32),
            pltpu.VMEM((d_ff // FFN_CHUNK * slabs, HALO_S, LANES), jnp.float32),
            pltpu.VMEM((t, d_ff), jnp.bfloat16),
        ],
        compiler_params=pltpu.CompilerParams(
            dimension_semantics=("arbitrary",),
            vmem_limit_bytes=VMEM_LIMIT_BYTES),
        name="ffn_final" if final_norm else "ffn",
    )(x, *consts)


def _sublane_bcast(w, rows=SUBLANES):
    k, c = w.shape
    return jnp.broadcast_to(w[:, None, :], (k, rows, c))


def kernel(x, mix_norm_g, w_in, b_in, conv_a_w, conv_a_b, ln_a_g, ln_a_b, conv_b_w, w_out,
           ffn_norm_g, w_up, conv_f_w, w_down, final_norm_g):
    depth = w_in.shape[0]
    bf16 = jnp.bfloat16
    w_in, w_out, w_up, w_down = (w.astype(bf16) for w in (w_in, w_out, w_up, w_down))
    for l in range(depth):
        x = _mixer_call(
            x, l, mix_norm_g[l][None], w_in, b_in[l][None],
            _sublane_bcast(conv_a_w[l].astype(bf16), 2 * SUBLANES), conv_a_b[l][None],
            ln_a_g[l][None], ln_a_b[l][None], _sublane_bcast(conv_b_w[l]), w_out)
        x = _ffn_call(x, l, ffn_norm_g[l][None], w_up, _sublane_bcast(conv_f_w[l]), w_down,
                      final_norm_g[None], final_norm=(l == depth - 1))
    return x
```
